```python
import math
import jax, jax.numpy as jnp
from jax import lax
import numpy as np

D_MODEL = 1024
BATCH = 1
SEQ = 16384
DEPTH = 1

CHUNK = 64
Q_BLOCK = 128
PLE_DIM = 256
ROPE_THETA = 10000.0
DA_HEADS = 4
DA_QK_DIM = 64
DA_V_DIM = 128
DA_QK_W = DA_HEADS * 2 * DA_QK_DIM
DA_V_W = DA_HEADS * DA_V_DIM
ML_HEADS = 4
ML_DIM = 128
ML_W = ML_HEADS * ML_DIM
ML_CONV = 4
D_FF = 2816
FF_CONV = 3
EPS = 1e-6
IN_SIZES = (DA_QK_W, DA_QK_W, DA_V_W, 2 * ML_W, ML_W, ML_W, 2 * ML_HEADS, D_MODEL, D_MODEL)
N_IN = DA_QK_W * 2 + DA_V_W + 2 * ML_W + ML_W + ML_W + 2 * ML_HEADS + 2 * D_MODEL

kernel_name = "diffattn_mlstm_gated_hybrid_block"


def rmsnorm(x, g):
    xf = x.astype(jnp.float32)
    y = xf * lax.rsqrt(jnp.mean(xf * xf, axis=-1, keepdims=True) + EPS)
    return (y * g.astype(jnp.float32)).astype(x.dtype)


def head_layernorm(x, g):
    xf = x.astype(jnp.float32)
    mu = jnp.mean(xf, axis=-1, keepdims=True)
    var = jnp.mean(jnp.square(xf - mu), axis=-1, keepdims=True)
    return ((xf - mu) * lax.rsqrt(var + EPS) * g.astype(jnp.float32)).astype(x.dtype)


def causal_dwconv(x, w):
    K, C = w.shape
    return lax.conv_general_dilated(x, w[:, None, :].astype(x.dtype), window_strides=(1,),
                                    padding=[(K - 1, 0)],
                                    dimension_numbers=('NWC', 'WIO', 'NWC'),
                                    feature_group_count=C)


def rope_tables(S, dim):
    inv_freq = ROPE_THETA ** (-jnp.arange(0, dim, 2, dtype=jnp.float32) / dim)
    ang = jnp.arange(S, dtype=jnp.float32)[:, None] * inv_freq[None, :]
    return jnp.cos(ang), jnp.sin(ang)


def apply_rope(t, cos, sin):
    tf = t.astype(jnp.float32)
    half = tf.shape[-1] // 2
    c = cos[None, :, None, None, :]
    s = sin[None, :, None, None, :]
    t1, t2 = tf[..., :half], tf[..., half:]
    return jnp.concatenate([t1 * c - t2 * s, t2 * c + t1 * s], axis=-1).astype(t.dtype)


def diff_attention(q, k, v, lam):
    B, S, H, _, dh = q.shape
    dv = v.shape[-1]
    nb = S // Q_BLOCK
    qb = q.reshape(B, nb, Q_BLOCK, H, 2, dh).transpose(1, 0, 2, 3, 4, 5)
    k_chunk = jnp.arange(S) // CHUNK
    scale = dh ** -0.5

    def block(args):
        q_blk, bi = args
        q_chunk = (bi * Q_BLOCK + jnp.arange(Q_BLOCK)) // CHUNK
        mask = k_chunk[None, :] <= q_chunk[:, None]
        s = jnp.einsum('bqhmd,bkhmd->bhmqk', q_blk, k).astype(jnp.float32) * scale
        s = jnp.where(mask, s, -jnp.inf)
        a = jax.nn.softmax(s, axis=-1)
        w = a[:, :, 0] - lam * a[:, :, 1]
        return jnp.einsum('bhqk,bkhe->bqhe', w.astype(v.dtype), v)

    out = lax.map(block, (qb, jnp.arange(nb)))
    return out.transpose(1, 0, 2, 3, 4).reshape(B, S, H, dv)


def mlstm_chunkwise(q, k, v, i_pre, logf):
    B, S, H, dk = q.shape
    dv = v.shape[-1]
    nc = S // CHUNK
    out_dtype = v.dtype

    def chunks(t):
        return t.astype(jnp.float32).reshape(B, nc, CHUNK, H, -1).transpose(1, 0, 3, 2, 4)

    qc, kc, vc = chunks(q), chunks(k), chunks(v)
    ic = chunks(i_pre[..., None])[..., 0]
    fc = chunks(logf[..., None])[..., 0]
    causal = jnp.tril(jnp.ones((CHUNK, CHUNK), dtype=bool))

    def step(carry, inp):
        C, n, m = carry
        q_, k_, v_, i_, f_ = inp
        b = jnp.cumsum(f_, axis=-1)
        D = b[..., :, None] - b[..., None, :] + i_[..., None, :]
        D = jnp.where(causal, D, -jnp.inf)
        inter = b + m[..., None]
        m_t = jnp.maximum(inter, jnp.max(D, axis=-1))
        w_intra = jnp.exp(D - m_t[..., None])
        w_inter = jnp.exp(inter - m_t)
        s = jnp.einsum('bhtd,bhsd->bhts', q_, k_) * w_intra
        num = (w_inter[..., None] * jnp.einsum('bhtd,bhde->bhte', q_, C)
               + jnp.einsum('bhts,bhse->bhte', s, v_))
        den = w_inter * jnp.einsum('bhtd,bhd->bht', q_, n) + jnp.sum(s, axis=-1)
        h = num / jnp.maximum(jnp.abs(den), jnp.exp(-m_t))[..., None]
        bL = b[..., -1]
        g = bL[..., None] - b + i_
        m_new = jnp.maximum(bL + m, jnp.max(g, axis=-1))
        wk = jnp.exp(g - m_new[..., None])
        decay = jnp.exp(bL + m - m_new)
        C_new = decay[..., None, None] * C + jnp.einsum('bhs,bhsd,bhse->bhde', wk, k_, v_)
        n_new = decay[..., None] * n + jnp.einsum('bhs,bhsd->bhd', wk, k_)
        return (C_new, n_new, m_new), h

    init = (jnp.zeros((B, H, dk, dv), jnp.float32), jnp.zeros((B, H, dk), jnp.float32),
            jnp.zeros((B, H), jnp.float32))
    _, hs = lax.scan(step, init, (qc, kc, vc, ic, fc))
    return hs.transpose(1, 0, 3, 2, 4).reshape(B, S, H, dv).astype(out_dtype)


def setup_inputs(seed: int = 0) -> dict:
    key = jax.random.key(seed)
    ks = jax.random.split(key, 24)
    nrm = lambda k, shape, fan: jax.random.normal(k, shape, jnp.float32) * fan ** -0.5
    gain = lambda k, shape: 1.0 + 0.02 * jax.random.normal(k, shape, jnp.float32)
    b_if = jnp.concatenate([
        0.1 * jax.random.normal(ks[3], (DEPTH, ML_HEADS), jnp.float32),
        jax.random.uniform(ks[4], (DEPTH, ML_HEADS), jnp.float32, 3.0, 6.0)], axis=-1)
    return {
        "x": jax.random.normal(ks[0], (BATCH, SEQ, D_MODEL), jnp.float32),
        "p": jax.random.normal(ks[1], (DEPTH, BATCH, SEQ, PLE_DIM), jnp.float32),
        "norm1_g": gain(ks[2], (DEPTH, D_MODEL)),
        "w_in": nrm(ks[5], (DEPTH, D_MODEL, N_IN), D_MODEL),
        "b_if": b_if,
        "conv_m_w": nrm(ks[6], (DEPTH, ML_CONV, 2 * ML_W), ML_CONV),
        "lam_q1": 0.1 * jax.random.normal(ks[7], (DEPTH, DA_QK_DIM), jnp.float32),
        "lam_k1": 0.1 * jax.random.normal(ks[8], (DEPTH, DA_QK_DIM), jnp.float32),
        "lam_q2": 0.1 * jax.random.normal(ks[9], (DEPTH, DA_QK_DIM), jnp.float32),
        "lam_k2": 0.1 * jax.random.normal(ks[10], (DEPTH, DA_QK_DIM), jnp.float32),
        "da_norm_g": gain(ks[11], (DEPTH, DA_V_DIM)),
        "ml_norm_g": gain(ks[12], (DEPTH, ML_DIM)),
        "w_ya": nrm(ks[13], (DEPTH, DA_V_W, D_MODEL), DA_V_W),
        "w_yb": nrm(ks[14], (DEPTH, ML_W, D_MODEL), ML_W),
        "w_o": nrm(ks[15], (DEPTH, D_MODEL, D_MODEL), D_MODEL),
        "norm2_g": gain(ks[16], (DEPTH, D_MODEL)),
        "w_up": nrm(ks[17], (DEPTH, D_MODEL, 2 * D_FF), D_MODEL),
        "conv_f_w": nrm(ks[18], (DEPTH, FF_CONV, D_FF), FF_CONV),
        "conv_f_b": 0.02 * jax.random.normal(ks[19], (DEPTH, D_FF), jnp.float32),
        "w_down": nrm(ks[20], (DEPTH, D_FF, D_MODEL), D_FF),
        "w_ple": nrm(ks[21], (DEPTH, PLE_DIM, D_MODEL), PLE_DIM),
        "w_pg": nrm(ks[22], (DEPTH, D_MODEL, D_MODEL), D_MODEL),
        "final_g": gain(ks[23], (D_MODEL,)),
    }


def reference(x, p, norm1_g, w_in, b_if, conv_m_w, lam_q1, lam_k1, lam_q2, lam_k2,
              da_norm_g, ml_norm_g, w_ya, w_yb, w_o, norm2_g, w_up, conv_f_w, conv_f_b,
              w_down, w_ple, w_pg, final_g):
    B, S, _ = x.shape
    cos, sin = rope_tables(S, DA_QK_DIM)
    split_pts = list(np.cumsum(IN_SIZES)[:-1])
    for i in range(DEPTH):
        lam_init = 0.8 - 0.6 * math.exp(-0.3 * i)
        h = rmsnorm(x, norm1_g[i])
        z = h @ w_in[i]
        qa, ka, va, qk_m, vm, om, if_pre, ga, gb = jnp.split(z, split_pts, axis=-1)

        qa = apply_rope(qa.reshape(B, S, DA_HEADS, 2, DA_QK_DIM), cos, sin)
        ka = apply_rope(ka.reshape(B, S, DA_HEADS, 2, DA_QK_DIM), cos, sin)
        va = va.reshape(B, S, DA_HEADS, DA_V_DIM)
        lam = (jnp.exp(jnp.sum(lam_q1[i].astype(jnp.float32) * lam_k1[i].astype(jnp.float32)))
               - jnp.exp(jnp.sum(lam_q2[i].astype(jnp.float32) * lam_k2[i].astype(jnp.float32)))
               + lam_init)
        ao = diff_attention(qa, ka, va, lam)
        ao = rmsnorm(ao, da_norm_g[i]) * (1.0 - lam_init)
        y_a = ao.reshape(B, S, DA_V_W) @ w_ya[i]

        qk_m = jax.nn.silu(causal_dwconv(qk_m, conv_m_w[i]))
        qm, km = jnp.split(qk_m, 2, axis=-1)
        gates = if_pre.astype(jnp.float32) + b_if[i].astype(jnp.float32)
        i_pre = gates[..., :ML_HEADS]
        logf = jax.nn.log_sigmoid(gates[..., ML_HEADS:])
        hm = mlstm_chunkwise(qm.reshape(B, S, ML_HEADS, ML_DIM),
                             km.reshape(B, S, ML_HEADS, ML_DIM) * (ML_DIM ** -0.5),
                             vm.reshape(B, S, ML_HEADS, ML_DIM), i_pre, logf)
        hm = head_layernorm(hm, ml_norm_g[i]) * jax.nn.sigmoid(om).reshape(B, S, ML_HEADS, ML_DIM)
        y_b = hm.reshape(B, S, ML_W) @ w_yb[i]

        merged = jax.nn.sigmoid(ga) * y_a + jax.nn.sigmoid(gb) * y_b
        x = x + merged @ w_o[i]

        h2 = rmsnorm(x, norm2_g[i])
        a, g = jnp.split(h2 @ w_up[i], 2, axis=-1)
        a = causal_dwconv(a, conv_f_w[i]) + conv_f_b[i]
        x = x + (jax.nn.gelu(a) * g) @ w_down[i]

        x = x + (p[i] @ w_ple[i]) * jax.nn.sigmoid(x @ w_pg[i])
    return rmsnorm(x, final_g)
```

```python
import functools
import math

import jax
import jax.numpy as jnp
from jax import lax
from jax.experimental import pallas as pl
from jax.experimental.pallas import tpu as pltpu

F32 = jnp.float32
BF16 = jnp.bfloat16

CHUNK = 64
ROPE_THETA = 10000.0
DA_HEADS = 4
DA_QK_DIM = 64
DA_V_DIM = 128
ML_HEADS = 4
ML_DIM = 128
EPS = 1e-6
LANES = 128
SUBLANES = 8
VMEM_LIMIT_BYTES = 56 * 1024 * 1024

PROJ_ROWS = 256
ATTN_TILE = 256
ML_BLOCK = 256
POST_ROWS = 256
FF_CHUNK = 256


def _dot(a, b):
    return jnp.dot(a, b, preferred_element_type=F32)


def _sigmoid(x):
    return 1.0 / (1.0 + jnp.exp(-x))


def _rmsnorm(x, g):
    return x * lax.rsqrt(jnp.mean(x * x, axis=-1, keepdims=True) + EPS) * g


def _const_spec(shape):
    nd = len(shape)
    return pl.BlockSpec(shape, lambda *_: (0,) * nd)


def _proj_kernel(x_ref, g1_ref, cos_ref, sin_ref, wqk_ref, wv_ref, wqkm_ref, wvm_ref, wom_ref,
                 wif_ref, wga_ref, wgb_ref, bif_ref, convw_ref,
                 qT_ref, k_ref, vT_ref, qm_ref, km_ref, vm_ref, om_ref, gates_ref, ga_ref, gb_ref,
                 carry_ref, cbuf_ref, *, tile):
    i = pl.program_id(0)
    tm = x_ref.shape[0]
    n_sub = tm // tile
    hb = _rmsnorm(x_ref[...], g1_ref[...]).astype(BF16)

    qk = _dot(hb, wqk_ref[...])
    cos = cos_ref[...]
    sin = sin_ref[...]
    lane = lax.broadcasted_iota(jnp.int32, (tm, LANES), 1)
    first_half = (lane % DA_QK_DIM) < (DA_QK_DIM // 2)
    n_grp = DA_HEADS
    for c in range(2 * n_grp):
        t = qk[:, c * LANES:(c + 1) * LANES]
        partner = jnp.where(first_half, pltpu.roll(t, LANES - DA_QK_DIM // 2, 1),
                            pltpu.roll(t, DA_QK_DIM // 2, 1))
        r = t * cos + partner * sin
        if c < n_grp:
            rT = (r * (DA_QK_DIM ** -0.5)).T.astype(BF16)
            for s in range(n_sub):
                qT_ref[c, s] = rT[:, s * tile:(s + 1) * tile]
        else:
            k_ref[:, (c - n_grp) * LANES:(c - n_grp + 1) * LANES] = r.astype(BF16)

    v = _dot(hb, wv_ref[...])
    for h in range(DA_HEADS):
        vT = v[:, h * LANES:(h + 1) * LANES].T.astype(BF16)
        for s in range(n_sub):
            vT_ref[h, s] = vT[:, s * tile:(s + 1) * tile]

    @pl.when(i == 0)
    def _():
        carry_ref[...] = jnp.zeros_like(carry_ref)

    raw = _dot(hb, wqkm_ref[...])
    cbuf_ref[0:SUBLANES, :] = carry_ref[...]
    cbuf_ref[SUBLANES:, :] = raw
    carry_ref[...] = raw[tm - SUBLANES:, :]
    cw = convw_ref[...]
    y = (cw[3:4, :] * raw
         + cw[2:3, :] * cbuf_ref[pl.ds(SUBLANES - 1, tm), :]
         + cw[1:2, :] * cbuf_ref[pl.ds(SUBLANES - 2, tm), :]
         + cw[0:1, :] * cbuf_ref[pl.ds(SUBLANES - 3, tm), :])
    y = y * _sigmoid(y)
    half = y.shape[1] // 2
    qm_ref[...] = y[:, :half].astype(BF16)
    km_ref[...] = (y[:, half:] * (ML_DIM ** -0.5)).astype(BF16)

    vm_ref[...] = _dot(hb, wvm_ref[...]).astype(BF16)
    om_ref[...] = _dot(hb, wom_ref[...]).astype(BF16)
    ga_ref[...] = _dot(hb, wga_ref[...]).astype(BF16)
    gb_ref[...] = _dot(hb, wgb_ref[...]).astype(BF16)

    gp = _dot(hb, wif_ref[...]) + bif_ref[...]
    logsig = jnp.minimum(gp, 0.0) - jnp.log1p(jnp.exp(-jnp.abs(gp)))
    gates_ref[...] = jnp.where(lane < ML_HEADS, gp, logsig)


def _attn_kernel(qT_ref, k_ref, vT_ref, lq1_ref, lk1_ref, lq2_ref, lk2_ref, g_ref, o_ref,
                 acc_ref, m_ref, l_ref, *, lam_init):
    i = pl.program_id(1)
    t = qT_ref.shape[3]
    qT = qT_ref[0, 0]
    row = lax.broadcasted_iota(jnp.int32, qT.shape, 0)
    zero = jnp.zeros_like(qT)
    q2 = jnp.concatenate([jnp.where(row < DA_QK_DIM, qT, zero),
                          jnp.where(row >= DA_QK_DIM, qT, zero)], axis=1)

    m_ref[...] = jnp.full_like(m_ref, -jnp.inf)
    l_ref[...] = jnp.zeros_like(l_ref)
    acc_ref[...] = jnp.zeros_like(acc_ref)

    def step(j, masked):
        kt = k_ref[pl.ds(pl.multiple_of(j * t, t), t), :]
        vt = vT_ref[0, j]
        s = _dot(kt, q2)
        if masked:
            kc = lax.broadcasted_iota(jnp.int32, s.shape, 0) // CHUNK
            qc = (lax.broadcasted_iota(jnp.int32, s.shape, 1) % t) // CHUNK
            s = jnp.where(kc <= qc, s, -jnp.inf)
        m_prev = m_ref[...]
        m_new = jnp.maximum(m_prev, jnp.max(s, axis=0, keepdims=True))
        alpha = jnp.exp(m_prev - m_new)
        p = jnp.exp(s - m_new)
        l_ref[...] = alpha * l_ref[...] + jnp.sum(p, axis=0, keepdims=True)
        acc_ref[...] = alpha * acc_ref[...] + _dot(vt, p.astype(BF16))
        m_ref[...] = m_new

    def body(j, carry):
        step(j, False)
        return carry

    lax.fori_loop(0, i, body, 0)
    step(i, True)

    o = acc_ref[...] / l_ref[...]
    lam = (jnp.exp(jnp.sum(lq1_ref[...] * lk1_ref[...], axis=-1, keepdims=True))
           - jnp.exp(jnp.sum(lq2_ref[...] * lk2_ref[...], axis=-1, keepdims=True))
           + lam_init)
    ao = o[:, :t] - lam * o[:, t:]
    y = ao * lax.rsqrt(jnp.mean(ao * ao, axis=0, keepdims=True) + EPS)
    o_ref[...] = (y.T * g_ref[...] * (1.0 - lam_init)).astype(o_ref.dtype)


def _mlstm_kernel(q_ref, k_ref, v_ref, om_ref, gates_ref, g_ref, o_ref, state_ref, m_ref):
    c = pl.program_id(0)
    L = q_ref.shape[0]

    @pl.when(c == 0)
    def _():
        state_ref[...] = jnp.zeros_like(state_ref)
        m_ref[...] = jnp.zeros_like(m_ref)

    gates = gates_ref[...]
    r = lax.broadcasted_iota(jnp.int32, (L, L), 0)
    cc = lax.broadcasted_iota(jnp.int32, (L, L), 1)
    tril = cc <= r
    cum = jnp.dot(tril.astype(F32), gates, preferred_element_type=F32,
                  precision=lax.Precision.HIGHEST)
    gatesT = gates.T
    cumT = cum.T
    lane = lax.broadcasted_iota(jnp.int32, (L, LANES), 1)
    ones_col = jnp.where(lane == 0, 1.0, 0.0).astype(BF16)

    for h in range(ML_HEADS):
        sl = slice(h * ML_DIM, (h + 1) * ML_DIM)
        b_col = cum[:, ML_HEADS + h:ML_HEADS + h + 1]
        b_row = cumT[ML_HEADS + h:ML_HEADS + h + 1, :]
        i_col = gates[:, h:h + 1]
        i_row = gatesT[h:h + 1, :]
        m_prev = m_ref[h, 0:1, 0:1]

        d = jnp.where(tril, b_col - b_row + i_row, -jnp.inf)
        inter = b_col + m_prev
        m_t = jnp.maximum(inter, jnp.max(d, axis=1, keepdims=True))
        w_intra = jnp.exp(d - m_t)
        w_inter = jnp.exp(inter - m_t)

        q = q_ref[:, sl]
        k = k_ref[:, sl]
        v_aug = jnp.concatenate([v_ref[:, sl], ones_col], axis=1)
        qk = lax.dot_general(q, k, (((1,), (1,)), ((), ())), preferred_element_type=F32)
        s = (qk * w_intra).astype(BF16)
        state = state_ref[h]
        nd = w_inter * _dot(q, state.astype(BF16)) + _dot(s, v_aug)
        num = nd[:, :ML_DIM]
        den = nd[:, ML_DIM:ML_DIM + 1]
        hh = num / jnp.maximum(jnp.abs(den), jnp.exp(-m_t))
        mu = jnp.mean(hh, axis=-1, keepdims=True)
        xc = hh - mu
        var = jnp.mean(xc * xc, axis=-1, keepdims=True)
        y = xc * lax.rsqrt(var + EPS) * g_ref[...]
        y = y * _sigmoid(om_ref[:, sl].astype(F32))
        o_ref[:, sl] = y.astype(o_ref.dtype)

        b_last = b_col[L - 1:L, :]
        g_col = b_last - b_col + i_col
        m_new = jnp.maximum(b_last + m_prev, jnp.max(g_col, axis=0, keepdims=True))
        wk = jnp.exp(g_col - m_new)
        decay = jnp.exp(b_last + m_prev - m_new)
        kw = (k.astype(F32) * wk).astype(BF16)
        upd = lax.dot_general(kw, v_aug, (((0,), (0,)), ((), ())), preferred_element_type=F32)
        state_ref[h] = decay * state + upd
        m_ref[h] = jnp.broadcast_to(m_new, m_ref.shape[1:])


def _gelu_tanh(x):
    return 0.5 * x * (1.0 + jnp.tanh(math.sqrt(2.0 / math.pi) * (x + 0.044715 * (x * x * x))))


def _post_kernel(x_ref, ao_ref, hm_ref, ga_ref, gb_ref, p_ref, wya_ref, wyb_ref, wo_ref, n2g_ref,
                 wupa_ref, wupg_ref, convw_ref, convb_ref, wdown_ref, wple_ref, wpg_ref, fg_ref,
                 o_ref, carry_ref, abuf_ref):
    i = pl.program_id(0)
    tm = x_ref.shape[0]
    n_chunks = wupa_ref.shape[0]

    @pl.when(i == 0)
    def _():
        carry_ref[...] = jnp.zeros_like(carry_ref)

    ya = _dot(ao_ref[...], wya_ref[...])
    yb = _dot(hm_ref[...], wyb_ref[...])
    merged = _sigmoid(ga_ref[...].astype(F32)) * ya + _sigmoid(gb_ref[...].astype(F32)) * yb
    x1 = x_ref[...] + _dot(merged.astype(BF16), wo_ref[...])

    h2 = _rmsnorm(x1, n2g_ref[...]).astype(BF16)
    acc = jnp.zeros(x1.shape, F32)
    for c in range(n_chunks):
        a = _dot(h2, wupa_ref[c])
        g = _dot(h2, wupg_ref[c])
        abuf_ref[0:SUBLANES, :] = carry_ref[c]
        abuf_ref[SUBLANES:, :] = a
        carry_ref[c] = a[tm - SUBLANES:, :]
        cw = convw_ref[c]
        ac = (cw[2:3, :] * a
              + cw[1:2, :] * abuf_ref[pl.ds(SUBLANES - 1, tm), :]
              + cw[0:1, :] * abuf_ref[pl.ds(SUBLANES - 2, tm), :]
              + convb_ref[c])
        u = (_gelu_tanh(ac) * g).astype(BF16)
        acc = acc + _dot(u, wdown_ref[c])
    x2 = x1 + acc

    pe = _dot(p_ref[...].astype(BF16), wple_ref[...])
    pg = _dot(x2.astype(BF16), wpg_ref[...])
    x3 = x2 + pe * _sigmoid(pg)
    o_ref[...] = _rmsnorm(x3, fg_ref[...])


def _compiler_params(n_axes):
    return pltpu.CompilerParams(dimension_semantics=("arbitrary",) * n_axes,
                                vmem_limit_bytes=VMEM_LIMIT_BYTES)


def _rope_tables(S):
    half = DA_QK_DIM // 2
    inv_freq = ROPE_THETA ** (-jnp.arange(0, DA_QK_DIM, 2, dtype=F32) / DA_QK_DIM)
    ang = jnp.arange(S, dtype=F32)[:, None] * inv_freq[None, :]
    cos, sin = jnp.cos(ang), jnp.sin(ang)
    reps = LANES // half
    cos_t = jnp.tile(cos, (1, reps))
    sin_t = jnp.concatenate([-sin, sin] * (reps // 2), axis=1)
    return cos_t, sin_t


def kernel(x, p, norm1_g, w_in, b_if, conv_m_w, lam_q1, lam_k1, lam_q2, lam_k2, da_norm_g,
           ml_norm_g, w_ya, w_yb, w_o, norm2_g, w_up, conv_f_w, conv_f_b, w_down, w_ple, w_pg,
           final_g):
    B, S, D = x.shape
    depth = w_in.shape[0]
    assert B == 1 and depth == 1
    d_ff = w_down.shape[1]
    qk_w = DA_HEADS * 2 * DA_QK_DIM
    v_w = DA_HEADS * DA_V_DIM
    ml_w = ML_HEADS * ML_DIM
    lam_init = 0.8 - 0.6 * math.exp(-0.3 * 0)

    tm1 = min(PROJ_ROWS, S)
    tile = min(ATTN_TILE, S)
    ml_blk = min(ML_BLOCK, S)
    tm4 = min(POST_ROWS, S)
    assert S % tm1 == 0 and S % tile == 0 and S % ml_blk == 0 and S % tm4 == 0 and tm1 % tile == 0
    assert d_ff % FF_CHUNK == 0
    n_tiles = S // tile

    x2d = x[0]
    wi = w_in[0]
    offs = [0]
    for sz in (qk_w, qk_w, v_w, 2 * ml_w, ml_w, ml_w, 2 * ML_HEADS, D, D):
        offs.append(offs[-1] + sz)
    piece = lambda n: wi[:, offs[n]:offs[n + 1]]
    w_qk = jnp.concatenate([piece(0), piece(1)], axis=1).astype(BF16)
    w_v = piece(2).astype(BF16)
    w_qkm = piece(3).astype(BF16)
    w_vm = piece(4).astype(BF16)
    w_om = piece(5).astype(BF16)
    w_if = jnp.pad(piece(6), ((0, 0), (0, LANES - 2 * ML_HEADS))).astype(BF16)
    w_ga = piece(7).astype(BF16)
    w_gb = piece(8).astype(BF16)
    bif = jnp.pad(b_if[0], (0, LANES - 2 * ML_HEADS))[None, :]
    cos_t, sin_t = _rope_tables(S)

    row1 = lambda a: a.reshape(1, -1)

    proj_out_shapes = (
        jax.ShapeDtypeStruct((DA_HEADS, n_tiles, LANES, tile), BF16),
        jax.ShapeDtypeStruct((S, qk_w), BF16),
        jax.ShapeDtypeStruct((DA_HEADS, n_tiles, LANES, tile), BF16),
        jax.ShapeDtypeStruct((S, ml_w), BF16),
        jax.ShapeDtypeStruct((S, ml_w), BF16),
        jax.ShapeDtypeStruct((S, ml_w), BF16),
        jax.ShapeDtypeStruct((S, ml_w), BF16),
        jax.ShapeDtypeStruct((S, LANES), F32),
        jax.ShapeDtypeStruct((S, D), BF16),
        jax.ShapeDtypeStruct((S, D), BF16),
    )
    rows = lambda w: pl.BlockSpec((tm1, w), lambda i: (i, 0))
    tspec = pl.BlockSpec((DA_HEADS, tm1 // tile, LANES, tile), lambda i: (0, i, 0, 0))
    qT, k_a, vT, qm, km, vm, om, gates, ga, gb = pl.pallas_call(
        functools.partial(_proj_kernel, tile=tile),
        grid=(S // tm1,),
        in_specs=[rows(D), _const_spec((1, D)), rows(LANES), rows(LANES),
                  _const_spec(w_qk.shape), _const_spec(w_v.shape), _const_spec(w_qkm.shape),
                  _const_spec(w_vm.shape), _const_spec(w_om.shape), _const_spec(w_if.shape),
                  _const_spec(w_ga.shape), _const_spec(w_gb.shape), _const_spec((1, LANES)),
                  _const_spec(conv_m_w.shape[1:])],
        out_specs=(tspec, rows(qk_w), tspec, rows(ml_w), rows(ml_w), rows(ml_w), rows(ml_w),
                   rows(LANES), rows(D), rows(D)),
        out_shape=proj_out_shapes,
        scratch_shapes=[pltpu.VMEM((SUBLANES, 2 * ml_w), F32),
                        pltpu.VMEM((tm1 + SUBLANES, 2 * ml_w), F32)],
        compiler_params=_compiler_params(1),
        name="proj",
    )(x2d, row1(norm1_g[0]), cos_t, sin_t, w_qk, w_v, w_qkm, w_vm, w_om, w_if, w_ga, w_gb, bif,
      conv_m_w[0])

    ao = pl.pallas_call(
        functools.partial(_attn_kernel, lam_init=lam_init),
        grid=(DA_HEADS, n_tiles),
        in_specs=[pl.BlockSpec((1, 1, LANES, tile), lambda h, i: (h, i, 0, 0)),
                  pl.BlockSpec((S, LANES), lambda h, i: (0, h)),
                  pl.BlockSpec((1, n_tiles, LANES, tile), lambda h, i: (h, 0, 0, 0)),
                  _const_spec((1, DA_QK_DIM)), _const_spec((1, DA_QK_DIM)),
                  _const_spec((1, DA_QK_DIM)), _const_spec((1, DA_QK_DIM)),
                  _const_spec((1, DA_V_DIM))],
        out_specs=pl.BlockSpec((tile, LANES), lambda h, i: (i, h)),
        out_shape=jax.ShapeDtypeStruct((S, v_w), BF16),
        scratch_shapes=[pltpu.VMEM((LANES, 2 * tile), F32),
                        pltpu.VMEM((1, 2 * tile), F32),
                        pltpu.VMEM((1, 2 * tile), F32)],
        compiler_params=_compiler_params(2),
        name="diff_attn",
    )(qT, k_a, vT, row1(lam_q1[0]), row1(lam_k1[0]), row1(lam_q2[0]), row1(lam_k2[0]),
      row1(da_norm_g[0]))

    mrows = lambda w: pl.BlockSpec((ml_blk, w), lambda c: (c, 0))
    hm = pl.pallas_call(
        _mlstm_kernel,
        grid=(S // ml_blk,),
        in_specs=[mrows(ml_w), mrows(ml_w), mrows(ml_w), mrows(ml_w), mrows(LANES),
                  _const_spec((1, ML_DIM))],
        out_specs=mrows(ml_w),
        out_shape=jax.ShapeDtypeStruct((S, ml_w), BF16),
        scratch_shapes=[pltpu.VMEM((ML_HEADS, ML_DIM, 2 * ML_DIM), F32),
                        pltpu.VMEM((ML_HEADS, SUBLANES, LANES), F32)],
        compiler_params=_compiler_params(1),
        name="mlstm",
    )(qm, km, vm, om, gates, row1(ml_norm_g[0]))

    n_chunks = d_ff // FF_CHUNK
    wu = w_up[0]
    chunk_cols = lambda w: w.reshape(w.shape[0], n_chunks, FF_CHUNK).transpose(1, 0, 2)
    w_up_a = chunk_cols(wu[:, :d_ff]).astype(BF16)
    w_up_g = chunk_cols(wu[:, d_ff:]).astype(BF16)
    conv_w = chunk_cols(conv_f_w[0])
    conv_b = conv_f_b[0].reshape(n_chunks, 1, FF_CHUNK)
    w_dn = w_down[0].reshape(n_chunks, FF_CHUNK, D).astype(BF16)
    prow = lambda w: pl.BlockSpec((tm4, w), lambda i: (i, 0))
    consts = (w_ya[0].astype(BF16), w_yb[0].astype(BF16), w_o[0].astype(BF16), row1(norm2_g[0]),
              w_up_a, w_up_g, conv_w, conv_b, w_dn, w_ple[0].astype(BF16), w_pg[0].astype(BF16),
              row1(final_g))
    out = pl.pallas_call(
        _post_kernel,
        grid=(S // tm4,),
        in_specs=[prow(D), prow(v_w), prow(ml_w), prow(D), prow(D), prow(p.shape[-1])]
                 + [_const_spec(a.shape) for a in consts],
        out_specs=prow(D),
        out_shape=jax.ShapeDtypeStruct((S, D), x.dtype),
        scratch_shapes=[pltpu.VMEM((n_chunks, SUBLANES, FF_CHUNK), F32),
                        pltpu.VMEM((tm4 + SUBLANES, FF_CHUNK), F32)],
        compiler_params=_compiler_params(1),
        name="post",
    )(x2d, ao, hm, ga, gb, p[0, 0], *consts)
    return out[None]
```

```python
import functools
import math

import jax
import jax.numpy as jnp
from jax import lax
from jax.experimental import pallas as pl
from jax.experimental.pallas import tpu as pltpu

F32 = jnp.float32
BF16 = jnp.bfloat16

CHUNK = 64
ROPE_THETA = 10000.0
DA_HEADS = 4
DA_QK_DIM = 64
DA_V_DIM = 128
ML_HEADS = 4
ML_DIM = 128
EPS = 1e-6
LOG2E = math.log2(math.e)
V_PAD_ROWS = 16
LANES = 128
SUBLANES = 8
VMEM_LIMIT_BYTES = 56 * 1024 * 1024

PROJ_ROWS = 256
ATTN_TILE = 256
ATTN_Q_TILES = 2
ML_BLOCK = 256
POST_ROWS = 256
FF_CHUNK = 256


def _dot(a, b):
    return jnp.dot(a, b, preferred_element_type=F32)


def _sigmoid(x):
    return 1.0 / (1.0 + jnp.exp(-x))


def _rmsnorm(x, g):
    return x * lax.rsqrt(jnp.mean(x * x, axis=-1, keepdims=True) + EPS) * g


def _const_spec(shape):
    nd = len(shape)
    return pl.BlockSpec(shape, lambda *_: (0,) * nd)


def _proj_kernel(x_ref, g1_ref, cos_ref, sin_ref, wqk_ref, wv_ref, wqkm_ref, wvm_ref, wom_ref,
                 wif_ref, wga_ref, wgb_ref, bif_ref, convw_ref,
                 qT_ref, k_ref, vT_ref, qm_ref, km_ref, vm_ref, om_ref, gates_ref, ga_ref, gb_ref,
                 carry_ref, cbuf_ref, *, tile):
    i = pl.program_id(0)
    tm = x_ref.shape[0]
    n_sub = tm // tile
    hb = _rmsnorm(x_ref[...], g1_ref[...]).astype(BF16)

    qk = _dot(hb, wqk_ref[...])
    cos = cos_ref[...]
    sin = sin_ref[...]
    lane = lax.broadcasted_iota(jnp.int32, (tm, LANES), 1)
    first_half = (lane % DA_QK_DIM) < (DA_QK_DIM // 2)
    n_grp = DA_HEADS
    for c in range(2 * n_grp):
        t = qk[:, c * LANES:(c + 1) * LANES]
        partner = jnp.where(first_half, pltpu.roll(t, LANES - DA_QK_DIM // 2, 1),
                            pltpu.roll(t, DA_QK_DIM // 2, 1))
        r = t * cos + partner * sin
        if c < n_grp:
            rT = (r * (DA_QK_DIM ** -0.5 * LOG2E)).T.astype(BF16)
            for s in range(n_sub):
                qT_ref[c, s] = rT[:, s * tile:(s + 1) * tile]
        else:
            k_ref[:, (c - n_grp) * LANES:(c - n_grp + 1) * LANES] = r.astype(BF16)

    v = _dot(hb, wv_ref[...])
    pad_row = lax.broadcasted_iota(jnp.int32, (V_PAD_ROWS, tm), 0)
    ones_rows = jnp.where(pad_row == 0, 1.0, 0.0).astype(BF16)
    for h in range(DA_HEADS):
        vT = jnp.concatenate([v[:, h * LANES:(h + 1) * LANES].T.astype(BF16), ones_rows], axis=0)
        for s in range(n_sub):
            vT_ref[h, s] = vT[:, s * tile:(s + 1) * tile]

    @pl.when(i == 0)
    def _():
        carry_ref[...] = jnp.zeros_like(carry_ref)

    raw = _dot(hb, wqkm_ref[...])
    cbuf_ref[0:SUBLANES, :] = carry_ref[...]
    cbuf_ref[SUBLANES:, :] = raw
    carry_ref[...] = raw[tm - SUBLANES:, :]
    cw = convw_ref[...]
    y = (cw[3:4, :] * raw
         + cw[2:3, :] * cbuf_ref[pl.ds(SUBLANES - 1, tm), :]
         + cw[1:2, :] * cbuf_ref[pl.ds(SUBLANES - 2, tm), :]
         + cw[0:1, :] * cbuf_ref[pl.ds(SUBLANES - 3, tm), :])
    y = y * _sigmoid(y)
    half = y.shape[1] // 2
    qm_ref[...] = y[:, :half].astype(BF16)
    km_ref[...] = (y[:, half:] * (ML_DIM ** -0.5)).astype(BF16)

    vm_ref[...] = _dot(hb, wvm_ref[...]).astype(BF16)
    om_ref[...] = _dot(hb, wom_ref[...]).astype(BF16)
    ga_ref[...] = _dot(hb, wga_ref[...]).astype(BF16)
    gb_ref[...] = _dot(hb, wgb_ref[...]).astype(BF16)

    gp = _dot(hb, wif_ref[...]) + bif_ref[...]
    logsig = jnp.minimum(gp, 0.0) - jnp.log1p(jnp.exp(-jnp.abs(gp)))
    gates_ref[...] = jnp.where(lane < ML_HEADS, gp, logsig)


def _attn_kernel(qT_ref, k_ref, vT_ref, lq1_ref, lk1_ref, lq2_ref, lk2_ref, g_ref, o_ref,
                 acc_ref, m_ref, sa_ref, sb_ref, *, lam_init):
    i = pl.program_id(1)
    t = qT_ref.shape[3]
    tq = ATTN_Q_TILES * t
    row = lax.broadcasted_iota(jnp.int32, (qT_ref.shape[2], t), 0)
    subs = [qT_ref[0, u] for u in range(ATTN_Q_TILES)]
    q2 = jnp.concatenate([jnp.where(row < DA_QK_DIM, q, jnp.zeros_like(q)) for q in subs]
                         + [jnp.where(row >= DA_QK_DIM, q, jnp.zeros_like(q)) for q in subs], axis=1)

    m_ref[...] = jnp.full_like(m_ref, -jnp.inf)
    acc_ref[...] = jnp.zeros_like(acc_ref)

    def scores(j, dst_ref):
        kt = k_ref[pl.ds(pl.multiple_of(j * t, t), t), :]
        dst_ref[...] = _dot(kt, q2)

    def update(src_ref, j, diag_tile):
        s = src_ref[...]
        if diag_tile is not None:
            kc = (lax.broadcasted_iota(jnp.int32, s.shape, 0) + diag_tile * t) // CHUNK
            qc = (lax.broadcasted_iota(jnp.int32, s.shape, 1) % tq) // CHUNK
            s = jnp.where(kc <= qc, s, -jnp.inf)
        m_prev = m_ref[...]
        m_new = jnp.maximum(m_prev, jnp.max(s, axis=0, keepdims=True))
        alpha = jnp.exp2(m_prev - m_new)
        p = jnp.exp2(s - m_new).astype(BF16)
        acc_ref[...] = alpha * acc_ref[...] + _dot(vT_ref[0, j], p)
        m_ref[...] = m_new

    scores(0, sa_ref)

    def pair(g, carry):
        j = 2 * g
        scores(j + 1, sb_ref)
        update(sa_ref, j, None)
        scores(j + 2, sa_ref)
        update(sb_ref, j + 1, None)
        return carry

    lax.fori_loop(0, i, pair, 0)
    scores(2 * i + 1, sb_ref)
    update(sa_ref, 2 * i, 0)
    update(sb_ref, 2 * i + 1, 1)

    acc = acc_ref[...]
    o = acc[:DA_V_DIM, :] / acc[DA_V_DIM:DA_V_DIM + 1, :]
    lam = (jnp.exp(jnp.sum(lq1_ref[...] * lk1_ref[...], axis=-1, keepdims=True))
           - jnp.exp(jnp.sum(lq2_ref[...] * lk2_ref[...], axis=-1, keepdims=True))
           + lam_init)
    ao = o[:, :tq] - lam * o[:, tq:]
    y = ao * lax.rsqrt(jnp.mean(ao * ao, axis=0, keepdims=True) + EPS)
    o_ref[...] = (y.T * g_ref[...] * (1.0 - lam_init)).astype(o_ref.dtype)


def _mlstm_kernel(q_ref, k_ref, v_ref, om_ref, gates_ref, g_ref, o_ref, state_ref, m_ref):
    c = pl.program_id(0)
    L = q_ref.shape[0]

    @pl.when(c == 0)
    def _():
        state_ref[...] = jnp.zeros_like(state_ref)
        m_ref[...] = jnp.zeros_like(m_ref)

    gates = gates_ref[...]
    r = lax.broadcasted_iota(jnp.int32, (L, L), 0)
    cc = lax.broadcasted_iota(jnp.int32, (L, L), 1)
    tril = cc <= r
    cum = jnp.dot(tril.astype(F32), gates, preferred_element_type=F32,
                  precision=lax.Precision.HIGHEST)
    gatesT = gates.T
    cumT = cum.T
    lane = lax.broadcasted_iota(jnp.int32, (L, LANES), 1)
    ones_col = jnp.where(lane == 0, 1.0, 0.0).astype(BF16)

    for h in range(ML_HEADS):
        sl = slice(h * ML_DIM, (h + 1) * ML_DIM)
        b_col = cum[:, ML_HEADS + h:ML_HEADS + h + 1]
        b_row = cumT[ML_HEADS + h:ML_HEADS + h + 1, :]
        i_col = gates[:, h:h + 1]
        i_row = gatesT[h:h + 1, :]
        m_prev = m_ref[h, 0:1, 0:1]

        d = jnp.where(tril, b_col - b_row + i_row, -jnp.inf)
        inter = b_col + m_prev
        m_t = jnp.maximum(inter, jnp.max(d, axis=1, keepdims=True))
        w_intra = jnp.exp(d - m_t)
        w_inter = jnp.exp(inter - m_t)

        q = q_ref[:, sl]
        k = k_ref[:, sl]
        v_aug = jnp.concatenate([v_ref[:, sl], ones_col], axis=1)
        qk = lax.dot_general(q, k, (((1,), (1,)), ((), ())), preferred_element_type=F32)
        s = (qk * w_intra).astype(BF16)
        state = state_ref[h]
        nd = w_inter * _dot(q, state.astype(BF16)) + _dot(s, v_aug)
        num = nd[:, :ML_DIM]
        den = nd[:, ML_DIM:ML_DIM + 1]
        hh = num / jnp.maximum(jnp.abs(den), jnp.exp(-m_t))
        mu = jnp.mean(hh, axis=-1, keepdims=True)
        xc = hh - mu
        var = jnp.mean(xc * xc, axis=-1, keepdims=True)
        y = xc * lax.rsqrt(var + EPS) * g_ref[...]
        y = y * _sigmoid(om_ref[:, sl].astype(F32))
        o_ref[:, sl] = y.astype(o_ref.dtype)

        b_last = b_col[L - 1:L, :]
        g_col = b_last - b_col + i_col
        m_new = jnp.maximum(b_last + m_prev, jnp.max(g_col, axis=0, keepdims=True))
        wk = jnp.exp(g_col - m_new)
        decay = jnp.exp(b_last + m_prev - m_new)
        kw = (k.astype(F32) * wk).astype(BF16)
        upd = lax.dot_general(kw, v_aug, (((0,), (0,)), ((), ())), preferred_element_type=F32)
        state_ref[h] = decay * state + upd
        m_ref[h] = jnp.broadcast_to(m_new, m_ref.shape[1:])


def _gelu_tanh(x):
    return 0.5 * x * (1.0 + jnp.tanh(math.sqrt(2.0 / math.pi) * (x + 0.044715 * (x * x * x))))


def _post_kernel(x_ref, ao_ref, hm_ref, ga_ref, gb_ref, p_ref, wya_ref, wyb_ref, wo_ref, n2g_ref,
                 wupa_ref, wupg_ref, convw_ref, convb_ref, wdown_ref, wple_ref, wpg_ref, fg_ref,
                 o_ref, carry_ref, abuf_ref):
    i = pl.program_id(0)
    tm = x_ref.shape[0]
    n_chunks = wupa_ref.shape[0]

    @pl.when(i == 0)
    def _():
        carry_ref[...] = jnp.zeros_like(carry_ref)

    ya = _dot(ao_ref[...], wya_ref[...])
    yb = _dot(hm_ref[...], wyb_ref[...])
    merged = _sigmoid(ga_ref[...].astype(F32)) * ya + _sigmoid(gb_ref[...].astype(F32)) * yb
    x1 = x_ref[...] + _dot(merged.astype(BF16), wo_ref[...])

    h2 = _rmsnorm(x1, n2g_ref[...]).astype(BF16)
    acc = jnp.zeros(x1.shape, F32)
    for c in range(n_chunks):
        a = _dot(h2, wupa_ref[c])
        g = _dot(h2, wupg_ref[c])
        abuf_ref[0:SUBLANES, :] = carry_ref[c]
        abuf_ref[SUBLANES:, :] = a
        carry_ref[c] = a[tm - SUBLANES:, :]
        cw = convw_ref[c]
        ac = (cw[2:3, :] * a
              + cw[1:2, :] * abuf_ref[pl.ds(SUBLANES - 1, tm), :]
              + cw[0:1, :] * abuf_ref[pl.ds(SUBLANES - 2, tm), :]
              + convb_ref[c])
        u = (_gelu_tanh(ac) * g).astype(BF16)
        acc = acc + _dot(u, wdown_ref[c])
    x2 = x1 + acc

    pe = _dot(p_ref[...].astype(BF16), wple_ref[...])
    pg = _dot(x2.astype(BF16), wpg_ref[...])
    x3 = x2 + pe * _sigmoid(pg)
    o_ref[...] = _rmsnorm(x3, fg_ref[...])


def _compiler_params(n_axes):
    return pltpu.CompilerParams(dimension_semantics=("arbitrary",) * n_axes,
                                vmem_limit_bytes=VMEM_LIMIT_BYTES)


def _rope_tables(S):
    half = DA_QK_DIM // 2
    inv_freq = ROPE_THETA ** (-jnp.arange(0, DA_QK_DIM, 2, dtype=F32) / DA_QK_DIM)
    ang = jnp.arange(S, dtype=F32)[:, None] * inv_freq[None, :]
    cos, sin = jnp.cos(ang), jnp.sin(ang)
    reps = LANES // half
    cos_t = jnp.tile(cos, (1, reps))
    sin_t = jnp.concatenate([-sin, sin] * (reps // 2), axis=1)
    return cos_t, sin_t


def kernel(x, p, norm1_g, w_in, b_if, conv_m_w, lam_q1, lam_k1, lam_q2, lam_k2, da_norm_g,
           ml_norm_g, w_ya, w_yb, w_o, norm2_g, w_up, conv_f_w, conv_f_b, w_down, w_ple, w_pg,
           final_g):
    B, S, D = x.shape
    depth = w_in.shape[0]
    assert B == 1 and depth == 1
    d_ff = w_down.shape[1]
    qk_w = DA_HEADS * 2 * DA_QK_DIM
    v_w = DA_HEADS * DA_V_DIM
    ml_w = ML_HEADS * ML_DIM
    lam_init = 0.8 - 0.6 * math.exp(-0.3 * 0)

    tm1 = min(PROJ_ROWS, S)
    tile = min(ATTN_TILE, S)
    ml_blk = min(ML_BLOCK, S)
    tm4 = min(POST_ROWS, S)
    assert S % tm1 == 0 and S % tile == 0 and S % ml_blk == 0 and S % tm4 == 0 and tm1 % tile == 0
    assert d_ff % FF_CHUNK == 0
    n_tiles = S // tile
    tq = ATTN_Q_TILES * tile
    assert S % tq == 0
    v_rows = DA_V_DIM + V_PAD_ROWS

    x2d = x[0]
    wi = w_in[0]
    offs = [0]
    for sz in (qk_w, qk_w, v_w, 2 * ml_w, ml_w, ml_w, 2 * ML_HEADS, D, D):
        offs.append(offs[-1] + sz)
    piece = lambda n: wi[:, offs[n]:offs[n + 1]]
    w_qk = jnp.concatenate([piece(0), piece(1)], axis=1).astype(BF16)
    w_v = piece(2).astype(BF16)
    w_qkm = piece(3).astype(BF16)
    w_vm = piece(4).astype(BF16)
    w_om = piece(5).astype(BF16)
    w_if = jnp.pad(piece(6), ((0, 0), (0, LANES - 2 * ML_HEADS))).astype(BF16)
    w_ga = piece(7).astype(BF16)
    w_gb = piece(8).astype(BF16)
    bif = jnp.pad(b_if[0], (0, LANES - 2 * ML_HEADS))[None, :]
    cos_t, sin_t = _rope_tables(S)

    row1 = lambda a: a.reshape(1, -1)

    proj_out_shapes = (
        jax.ShapeDtypeStruct((DA_HEADS, n_tiles, LANES, tile), BF16),
        jax.ShapeDtypeStruct((S, qk_w), BF16),
        jax.ShapeDtypeStruct((DA_HEADS, n_tiles, v_rows, tile), BF16),
        jax.ShapeDtypeStruct((S, ml_w), BF16),
        jax.ShapeDtypeStruct((S, ml_w), BF16),
        jax.ShapeDtypeStruct((S, ml_w), BF16),
        jax.ShapeDtypeStruct((S, ml_w), BF16),
        jax.ShapeDtypeStruct((S, LANES), F32),
        jax.ShapeDtypeStruct((S, D), BF16),
        jax.ShapeDtypeStruct((S, D), BF16),
    )
    rows = lambda w: pl.BlockSpec((tm1, w), lambda i: (i, 0))
    tspec = pl.BlockSpec((DA_HEADS, tm1 // tile, LANES, tile), lambda i: (0, i, 0, 0))
    vspec = pl.BlockSpec((DA_HEADS, tm1 // tile, v_rows, tile), lambda i: (0, i, 0, 0))
    qT, k_a, vT, qm, km, vm, om, gates, ga, gb = pl.pallas_call(
        functools.partial(_proj_kernel, tile=tile),
        grid=(S // tm1,),
        in_specs=[rows(D), _const_spec((1, D)), rows(LANES), rows(LANES),
                  _const_spec(w_qk.shape), _const_spec(w_v.shape), _const_spec(w_qkm.shape),
                  _const_spec(w_vm.shape), _const_spec(w_om.shape), _const_spec(w_if.shape),
                  _const_spec(w_ga.shape), _const_spec(w_gb.shape), _const_spec((1, LANES)),
                  _const_spec(conv_m_w.shape[1:])],
        out_specs=(tspec, rows(qk_w), vspec, rows(ml_w), rows(ml_w), rows(ml_w), rows(ml_w),
                   rows(LANES), rows(D), rows(D)),
        out_shape=proj_out_shapes,
        scratch_shapes=[pltpu.VMEM((SUBLANES, 2 * ml_w), F32),
                        pltpu.VMEM((tm1 + SUBLANES, 2 * ml_w), F32)],
        compiler_params=_compiler_params(1),
        name="proj",
    )(x2d, row1(norm1_g[0]), cos_t, sin_t, w_qk, w_v, w_qkm, w_vm, w_om, w_if, w_ga, w_gb, bif,
      conv_m_w[0])

    ao = pl.pallas_call(
        functools.partial(_attn_kernel, lam_init=lam_init),
        grid=(DA_HEADS, S // tq),
        in_specs=[pl.BlockSpec((1, ATTN_Q_TILES, LANES, tile), lambda h, i: (h, i, 0, 0)),
                  pl.BlockSpec((S, LANES), lambda h, i: (0, h)),
                  pl.BlockSpec((1, n_tiles, v_rows, tile), lambda h, i: (h, 0, 0, 0)),
                  _const_spec((1, DA_QK_DIM)), _const_spec((1, DA_QK_DIM)),
                  _const_spec((1, DA_QK_DIM)), _const_spec((1, DA_QK_DIM)),
                  _const_spec((1, DA_V_DIM))],
        out_specs=pl.BlockSpec((tq, LANES), lambda h, i: (i, h)),
        out_shape=jax.ShapeDtypeStruct((S, v_w), BF16),
        scratch_shapes=[pltpu.VMEM((v_rows, 2 * tq), F32),
                        pltpu.VMEM((1, 2 * tq), F32),
                        pltpu.VMEM((tile, 2 * tq), F32),
                        pltpu.VMEM((tile, 2 * tq), F32)],
        compiler_params=_compiler_params(2),
        name="diff_attn",
    )(qT, k_a, vT, row1(lam_q1[0]), row1(lam_k1[0]), row1(lam_q2[0]), row1(lam_k2[0]),
      row1(da_norm_g[0]))

    mrows = lambda w: pl.BlockSpec((ml_blk, w), lambda c: (c, 0))
    hm = pl.pallas_call(
        _mlstm_kernel,
        grid=(S // ml_blk,),
        in_specs=[mrows(ml_w), mrows(ml_w), mrows(ml_w), mrows(ml_w), mrows(LANES),
                  _const_spec((1, ML_DIM))],
        out_specs=mrows(ml_w),
        out_shape=jax.ShapeDtypeStruct((S, ml_w), BF16),
        scratch_shapes=[pltpu.VMEM((ML_HEADS, ML_DIM, 2 * ML_DIM), F32),
                        pltpu.VMEM((ML_HEADS, SUBLANES, LANES), F32)],
        compiler_params=_compiler_params(1),
        name="mlstm",
    )(qm, km, vm, om, gates, row1(ml_norm_g[0]))

    n_chunks = d_ff // FF_CHUNK
    wu = w_up[0]
    chunk_cols = lambda w: w.reshape(w.shape[0], n_chunks, FF_CHUNK).transpose(1, 0, 2)
    w_up_a = chunk_cols(wu[:, :d_ff]).astype(BF16)
    w_up_g = chunk_cols(wu[:, d_ff:]).astype(BF16)
    conv_w = chunk_cols(conv_f_w[0])
    conv_b = conv_f_b[0].reshape(n_chunks, 1, FF_CHUNK)
    w_dn = w_down[0].reshape(n_chunks, FF_CHUNK, D).astype(BF16)
    prow = lambda w: pl.BlockSpec((tm4, w), lambda i: (i, 0))
    consts = (w_ya[0].astype(BF16), w_yb[0].astype(BF16), w_o[0].astype(BF16), row1(norm2_g[0]),
              w_up_a, w_up_g, conv_w, conv_b, w_dn, w_ple[0].astype(BF16), w_pg[0].astype(BF16),
              row1(final_g))
    out = pl.pallas_call(
        _post_kernel,
        grid=(S // tm4,),
        in_specs=[prow(D), prow(v_w), prow(ml_w), prow(D), prow(D), prow(p.shape[-1])]
                 + [_const_spec(a.shape) for a in consts],
        out_specs=prow(D),
        out_shape=jax.ShapeDtypeStruct((S, D), x.dtype),
        scratch_shapes=[pltpu.VMEM((n_chunks, SUBLANES, FF_CHUNK), F32),
                        pltpu.VMEM((tm4 + SUBLANES, FF_CHUNK), F32)],
        compiler_params=_compiler_params(1),
        name="post",
    )(x2d, ao, hm, ga, gb, p[0, 0], *consts)
    return out[None]
```

```python
import functools
import math

import jax
import jax.numpy as jnp
from jax import lax
from jax.experimental import pallas as pl
from jax.experimental.pallas import tpu as pltpu

F32 = jnp.float32
BF16 = jnp.bfloat16

CHUNK = 64
ROPE_THETA = 10000.0
DA_HEADS = 4
DA_QK_DIM = 64
DA_V_DIM = 128
ML_HEADS = 4
ML_DIM = 128
EPS = 1e-6
LOG2E = math.log2(math.e)
V_PAD_ROWS = 16
LANES = 128
SUBLANES = 8
VMEM_LIMIT_BYTES = 56 * 1024 * 1024

PROJ_ROWS = 512
ATTN_TILE = 256
ATTN_Q_TILES = 2
ML_BLOCK = 256
POST_ROWS = 512
FF_CHUNK = 256


def _dot(a, b):
    return jnp.dot(a, b, preferred_element_type=F32)


def _sigmoid(x):
    return 1.0 / (1.0 + jnp.exp(-x))


def _rmsnorm(x, g):
    return x * lax.rsqrt(jnp.mean(x * x, axis=-1, keepdims=True) + EPS) * g


def _const_spec(shape):
    nd = len(shape)
    return pl.BlockSpec(shape, lambda *_: (0,) * nd)


def _proj_kernel(x_ref, g1_ref, cos_ref, sin_ref, wqk_ref, wv_ref, wqkm_ref, wvm_ref, wom_ref,
                 wif_ref, wga_ref, wgb_ref, bif_ref, convw_ref,
                 qT_ref, k_ref, vT_ref, qm_ref, km_ref, vm_ref, om_ref, gates_ref, ga_ref, gb_ref,
                 carry_ref, cbuf_ref, *, tile):
    i = pl.program_id(0)
    tm = x_ref.shape[0]
    n_sub = tm // tile
    hb = _rmsnorm(x_ref[...], g1_ref[...]).astype(BF16)

    qk = _dot(hb, wqk_ref[...])
    cos = cos_ref[...]
    sin = sin_ref[...]
    lane = lax.broadcasted_iota(jnp.int32, (tm, LANES), 1)
    first_half = (lane % DA_QK_DIM) < (DA_QK_DIM // 2)
    n_grp = DA_HEADS
    for c in range(2 * n_grp):
        t = qk[:, c * LANES:(c + 1) * LANES]
        partner = jnp.where(first_half, pltpu.roll(t, LANES - DA_QK_DIM // 2, 1),
                            pltpu.roll(t, DA_QK_DIM // 2, 1))
        r = t * cos + partner * sin
        if c < n_grp:
            rT = (r * (DA_QK_DIM ** -0.5 * LOG2E)).T.astype(BF16)
            for s in range(n_sub):
                qT_ref[c, s] = rT[:, s * tile:(s + 1) * tile]
        else:
            k_ref[:, (c - n_grp) * LANES:(c - n_grp + 1) * LANES] = r.astype(BF16)

    v = _dot(hb, wv_ref[...])
    pad_row = lax.broadcasted_iota(jnp.int32, (V_PAD_ROWS, tm), 0)
    ones_rows = jnp.where(pad_row == 0, 1.0, 0.0).astype(BF16)
    for h in range(DA_HEADS):
        vT = jnp.concatenate([v[:, h * LANES:(h + 1) * LANES].T.astype(BF16), ones_rows], axis=0)
        for s in range(n_sub):
            vT_ref[h, s] = vT[:, s * tile:(s + 1) * tile]

    @pl.when(i == 0)
    def _():
        carry_ref[...] = jnp.zeros_like(carry_ref)

    raw = _dot(hb, wqkm_ref[...])
    cbuf_ref[0:SUBLANES, :] = carry_ref[...]
    cbuf_ref[SUBLANES:, :] = raw
    carry_ref[...] = raw[tm - SUBLANES:, :]
    cw = convw_ref[...]
    y = (cw[3:4, :] * raw
         + cw[2:3, :] * cbuf_ref[pl.ds(SUBLANES - 1, tm), :]
         + cw[1:2, :] * cbuf_ref[pl.ds(SUBLANES - 2, tm), :]
         + cw[0:1, :] * cbuf_ref[pl.ds(SUBLANES - 3, tm), :])
    y = y * _sigmoid(y)
    half = y.shape[1] // 2
    qm_ref[...] = y[:, :half].astype(BF16)
    km_ref[...] = (y[:, half:] * (ML_DIM ** -0.5)).astype(BF16)

    vm_ref[...] = _dot(hb, wvm_ref[...]).astype(BF16)
    om_ref[...] = _dot(hb, wom_ref[...]).astype(BF16)
    ga_ref[...] = _dot(hb, wga_ref[...]).astype(BF16)
    gb_ref[...] = _dot(hb, wgb_ref[...]).astype(BF16)

    gp = _dot(hb, wif_ref[...]) + bif_ref[...]
    logsig = jnp.minimum(gp, 0.0) - jnp.log1p(jnp.exp(-jnp.abs(gp)))
    gates_ref[...] = jnp.where(lane < ML_HEADS, gp, logsig)


def _attn_kernel(qT_ref, k_ref, vT_ref, lq1_ref, lk1_ref, lq2_ref, lk2_ref, g_ref, o_ref,
                 *scratch, lam_init):
    nq = ATTN_Q_TILES
    assert nq == 2
    n_grp = 2 * nq
    acc_refs = scratch[0:n_grp]
    m_refs = scratch[n_grp:2 * n_grp]
    sa_refs = scratch[2 * n_grp:3 * n_grp]
    sb_refs = scratch[3 * n_grp:4 * n_grp]
    i = pl.program_id(1)
    t = qT_ref.shape[3]
    row = lax.broadcasted_iota(jnp.int32, (qT_ref.shape[2], t), 0)
    q_grp = []
    for mp in range(2):
        keep = (row < DA_QK_DIM) if mp == 0 else (row >= DA_QK_DIM)
        for u in range(nq):
            q = qT_ref[0, u]
            q_grp.append(jnp.where(keep, q, jnp.zeros_like(q)))

    for gi in range(n_grp):
        m_refs[gi][...] = jnp.full_like(m_refs[gi], -jnp.inf)
        acc_refs[gi][...] = jnp.zeros_like(acc_refs[gi])

    def scores(j, bufs, diag):
        kt = k_ref[pl.ds(pl.multiple_of(j * t, t), t), :]
        for gi in range(n_grp):
            if diag is None or gi % nq >= diag:
                bufs[gi][...] = _dot(kt, q_grp[gi])

    def update(bufs, j, diag):
        vt = vT_ref[0, j]
        for gi in range(n_grp):
            u = gi % nq
            if diag is not None and u < diag:
                continue
            s = bufs[gi][...]
            if diag is not None and u == diag:
                kc = lax.broadcasted_iota(jnp.int32, s.shape, 0) // CHUNK
                qc = lax.broadcasted_iota(jnp.int32, s.shape, 1) // CHUNK
                s = jnp.where(kc <= qc, s, -jnp.inf)
            m_prev = m_refs[gi][...]
            m_new = jnp.maximum(m_prev, jnp.max(s, axis=0, keepdims=True))
            alpha = jnp.exp2(m_prev - m_new)
            p = jnp.exp2(s - m_new).astype(BF16)
            acc_refs[gi][...] = alpha * acc_refs[gi][...] + _dot(vt, p)
            m_refs[gi][...] = m_new

    scores(0, sa_refs, None)

    def pair(j):
        scores(j + 1, sb_refs, None)
        update(sa_refs, j, None)
        scores(j + 2, sa_refs, None)
        update(sb_refs, j + 1, None)

    def quad(g, carry):
        pair(4 * g)
        pair(4 * g + 2)
        return carry

    lax.fori_loop(0, i // 2, quad, 0)

    @pl.when(i % 2 == 1)
    def _():
        pair(2 * (i - 1))

    scores(2 * i + 1, sb_refs, 1)
    update(sa_refs, 2 * i, 0)
    update(sb_refs, 2 * i + 1, 1)

    lam = (jnp.exp(jnp.sum(lq1_ref[...] * lk1_ref[...], axis=-1, keepdims=True))
           - jnp.exp(jnp.sum(lq2_ref[...] * lk2_ref[...], axis=-1, keepdims=True))
           + lam_init)
    for u in range(nq):
        a1 = acc_refs[u][...]
        a2 = acc_refs[nq + u][...]
        ao = (a1[:DA_V_DIM, :] / a1[DA_V_DIM:DA_V_DIM + 1, :]
              - lam * (a2[:DA_V_DIM, :] / a2[DA_V_DIM:DA_V_DIM + 1, :]))
        y = ao * lax.rsqrt(jnp.mean(ao * ao, axis=0, keepdims=True) + EPS)
        o_ref[u * t:(u + 1) * t, :] = (y.T * g_ref[...] * (1.0 - lam_init)).astype(o_ref.dtype)


def _mlstm_kernel(q_ref, k_ref, v_ref, om_ref, gates_ref, g_ref, o_ref, state_ref, m_ref):
    c = pl.program_id(0)
    L = q_ref.shape[0]

    @pl.when(c == 0)
    def _():
        state_ref[...] = jnp.zeros_like(state_ref)
        m_ref[...] = jnp.zeros_like(m_ref)

    gates = gates_ref[...]
    r = lax.broadcasted_iota(jnp.int32, (L, L), 0)
    cc = lax.broadcasted_iota(jnp.int32, (L, L), 1)
    tril = cc <= r
    cum = jnp.dot(tril.astype(F32), gates, preferred_element_type=F32,
                  precision=lax.Precision.HIGHEST)
    gatesT = gates.T
    cumT = cum.T
    lane = lax.broadcasted_iota(jnp.int32, (L, LANES), 1)
    ones_col = jnp.where(lane == 0, 1.0, 0.0).astype(BF16)

    for h in range(ML_HEADS):
        sl = slice(h * ML_DIM, (h + 1) * ML_DIM)
        b_col = cum[:, ML_HEADS + h:ML_HEADS + h + 1]
        b_row = cumT[ML_HEADS + h:ML_HEADS + h + 1, :]
        i_col = gates[:, h:h + 1]
        i_row = gatesT[h:h + 1, :]
        m_prev = m_ref[h, 0:1, 0:1]

        d = jnp.where(tril, b_col - b_row + i_row, -jnp.inf)
        inter = b_col + m_prev
        m_t = jnp.maximum(inter, jnp.max(d, axis=1, keepdims=True))
        w_intra = jnp.exp(d - m_t)
        w_inter = jnp.exp(inter - m_t)

        q = q_ref[:, sl]
        k = k_ref[:, sl]
        v_aug = jnp.concatenate([v_ref[:, sl], ones_col], axis=1)
        qk = lax.dot_general(q, k, (((1,), (1,)), ((), ())), preferred_element_type=F32)
        s = (qk * w_intra).astype(BF16)
        state = state_ref[h]
        nd = w_inter * _dot(q, state.astype(BF16)) + _dot(s, v_aug)
        num = nd[:, :ML_DIM]
        den = nd[:, ML_DIM:ML_DIM + 1]
        hh = num / jnp.maximum(jnp.abs(den), jnp.exp(-m_t))
        mu = jnp.mean(hh, axis=-1, keepdims=True)
        xc = hh - mu
        var = jnp.mean(xc * xc, axis=-1, keepdims=True)
        y = xc * lax.rsqrt(var + EPS) * g_ref[...]
        y = y * _sigmoid(om_ref[:, sl].astype(F32))
        o_ref[:, sl] = y.astype(o_ref.dtype)

        b_last = b_col[L - 1:L, :]
        g_col = b_last - b_col + i_col
        m_new = jnp.maximum(b_last + m_prev, jnp.max(g_col, axis=0, keepdims=True))
        wk = jnp.exp(g_col - m_new)
        decay = jnp.exp(b_last + m_prev - m_new)
        kw = (k.astype(F32) * wk).astype(BF16)
        upd = lax.dot_general(kw, v_aug, (((0,), (0,)), ((), ())), preferred_element_type=F32)
        state_ref[h] = decay * state + upd
        m_ref[h] = jnp.broadcast_to(m_new, m_ref.shape[1:])


def _gelu_tanh(x):
    return 0.5 * x * (1.0 + jnp.tanh(math.sqrt(2.0 / math.pi) * (x + 0.044715 * (x * x * x))))


def _post_kernel(x_ref, ao_ref, hm_ref, ga_ref, gb_ref, p_ref, wya_ref, wyb_ref, wo_ref, n2g_ref,
                 wupa_ref, wupg_ref, convw_ref, convb_ref, wdown_ref, wple_ref, wpg_ref, fg_ref,
                 o_ref, carry_ref, abuf_ref):
    i = pl.program_id(0)
    tm = x_ref.shape[0]
    n_chunks = wupa_ref.shape[0]

    @pl.when(i == 0)
    def _():
        carry_ref[...] = jnp.zeros_like(carry_ref)

    ya = _dot(ao_ref[...], wya_ref[...])
    yb = _dot(hm_ref[...], wyb_ref[...])
    merged = _sigmoid(ga_ref[...].astype(F32)) * ya + _sigmoid(gb_ref[...].astype(F32)) * yb
    x1 = x_ref[...] + _dot(merged.astype(BF16), wo_ref[...])

    h2 = _rmsnorm(x1, n2g_ref[...]).astype(BF16)
    acc = jnp.zeros(x1.shape, F32)
    for c in range(n_chunks):
        a = _dot(h2, wupa_ref[c])
        g = _dot(h2, wupg_ref[c])
        abuf_ref[0:SUBLANES, :] = carry_ref[c]
        abuf_ref[SUBLANES:, :] = a
        carry_ref[c] = a[tm - SUBLANES:, :]
        cw = convw_ref[c]
        ac = (cw[2:3, :] * a
              + cw[1:2, :] * abuf_ref[pl.ds(SUBLANES - 1, tm), :]
              + cw[0:1, :] * abuf_ref[pl.ds(SUBLANES - 2, tm), :]
              + convb_ref[c])
        u = (_gelu_tanh(ac) * g).astype(BF16)
        acc = acc + _dot(u, wdown_ref[c])
    x2 = x1 + acc

    pe = _dot(p_ref[...].astype(BF16), wple_ref[...])
    pg = _dot(x2.astype(BF16), wpg_ref[...])
    x3 = x2 + pe * _sigmoid(pg)
    o_ref[...] = _rmsnorm(x3, fg_ref[...])


def _compiler_params(n_axes):
    return pltpu.CompilerParams(dimension_semantics=("arbitrary",) * n_axes,
                                vmem_limit_bytes=VMEM_LIMIT_BYTES)


def _rope_tables(S):
    half = DA_QK_DIM // 2
    inv_freq = ROPE_THETA ** (-jnp.arange(0, DA_QK_DIM, 2, dtype=F32) / DA_QK_DIM)
    ang = jnp.arange(S, dtype=F32)[:, None] * inv_freq[None, :]
    cos, sin = jnp.cos(ang), jnp.sin(ang)
    reps = LANES // half
    cos_t = jnp.tile(cos, (1, reps))
    sin_t = jnp.concatenate([-sin, sin] * (reps // 2), axis=1)
    return cos_t, sin_t


def kernel(x, p, norm1_g, w_in, b_if, conv_m_w, lam_q1, lam_k1, lam_q2, lam_k2, da_norm_g,
           ml_norm_g, w_ya, w_yb, w_o, norm2_g, w_up, conv_f_w, conv_f_b, w_down, w_ple, w_pg,
           final_g):
    B, S, D = x.shape
    depth = w_in.shape[0]
    assert B == 1 and depth == 1
    d_ff = w_down.shape[1]
    qk_w = DA_HEADS * 2 * DA_QK_DIM
    v_w = DA_HEADS * DA_V_DIM
    ml_w = ML_HEADS * ML_DIM
    lam_init = 0.8 - 0.6 * math.exp(-0.3 * 0)

    tm1 = min(PROJ_ROWS, S)
    tile = min(ATTN_TILE, S)
    ml_blk = min(ML_BLOCK, S)
    tm4 = min(POST_ROWS, S)
    assert S % tm1 == 0 and S % tile == 0 and S % ml_blk == 0 and S % tm4 == 0 and tm1 % tile == 0
    assert d_ff % FF_CHUNK == 0
    n_tiles = S // tile
    tq = ATTN_Q_TILES * tile
    n_grp = 2 * ATTN_Q_TILES
    assert S % tq == 0
    v_rows = DA_V_DIM + V_PAD_ROWS

    x2d = x[0]
    wi = w_in[0]
    offs = [0]
    for sz in (qk_w, qk_w, v_w, 2 * ml_w, ml_w, ml_w, 2 * ML_HEADS, D, D):
        offs.append(offs[-1] + sz)
    piece = lambda n: wi[:, offs[n]:offs[n + 1]]
    w_qk = jnp.concatenate([piece(0), piece(1)], axis=1).astype(BF16)
    w_v = piece(2).astype(BF16)
    w_qkm = piece(3).astype(BF16)
    w_vm = piece(4).astype(BF16)
    w_om = piece(5).astype(BF16)
    w_if = jnp.pad(piece(6), ((0, 0), (0, LANES - 2 * ML_HEADS))).astype(BF16)
    w_ga = piece(7).astype(BF16)
    w_gb = piece(8).astype(BF16)
    bif = jnp.pad(b_if[0], (0, LANES - 2 * ML_HEADS))[None, :]
    cos_t, sin_t = _rope_tables(S)

    row1 = lambda a: a.reshape(1, -1)

    proj_out_shapes = (
        jax.ShapeDtypeStruct((DA_HEADS, n_tiles, LANES, tile), BF16),
        jax.ShapeDtypeStruct((S, qk_w), BF16),
        jax.ShapeDtypeStruct((DA_HEADS, n_tiles, v_rows, tile), BF16),
        jax.ShapeDtypeStruct((S, ml_w), BF16),
        jax.ShapeDtypeStruct((S, ml_w), BF16),
        jax.ShapeDtypeStruct((S, ml_w), BF16),
        jax.ShapeDtypeStruct((S, ml_w), BF16),
        jax.ShapeDtypeStruct((S, LANES), F32),
        jax.ShapeDtypeStruct((S, D), BF16),
        jax.ShapeDtypeStruct((S, D), BF16),
    )
    rows = lambda w: pl.BlockSpec((tm1, w), lambda i: (i, 0))
    tspec = pl.BlockSpec((DA_HEADS, tm1 // tile, LANES, tile), lambda i: (0, i, 0, 0))
    vspec = pl.BlockSpec((DA_HEADS, tm1 // tile, v_rows, tile), lambda i: (0, i, 0, 0))
    qT, k_a, vT, qm, km, vm, om, gates, ga, gb = pl.pallas_call(
        functools.partial(_proj_kernel, tile=tile),
        grid=(S // tm1,),
        in_specs=[rows(D), _const_spec((1, D)), rows(LANES), rows(LANES),
                  _const_spec(w_qk.shape), _const_spec(w_v.shape), _const_spec(w_qkm.shape),
                  _const_spec(w_vm.shape), _const_spec(w_om.shape), _const_spec(w_if.shape),
                  _const_spec(w_ga.shape), _const_spec(w_gb.shape), _const_spec((1, LANES)),
                  _const_spec(conv_m_w.shape[1:])],
        out_specs=(tspec, rows(qk_w), vspec, rows(ml_w), rows(ml_w), rows(ml_w), rows(ml_w),
                   rows(LANES), rows(D), rows(D)),
        out_shape=proj_out_shapes,
        scratch_shapes=[pltpu.VMEM((SUBLANES, 2 * ml_w), F32),
                        pltpu.VMEM((tm1 + SUBLANES, 2 * ml_w), F32)],
        compiler_params=_compiler_params(1),
        name="proj",
    )(x2d, row1(norm1_g[0]), cos_t, sin_t, w_qk, w_v, w_qkm, w_vm, w_om, w_if, w_ga, w_gb, bif,
      conv_m_w[0])

    ao = pl.pallas_call(
        functools.partial(_attn_kernel, lam_init=lam_init),
        grid=(DA_HEADS, S // tq),
        in_specs=[pl.BlockSpec((1, ATTN_Q_TILES, LANES, tile), lambda h, i: (h, i, 0, 0)),
                  pl.BlockSpec((S, LANES), lambda h, i: (0, h)),
                  pl.BlockSpec((1, n_tiles, v_rows, tile), lambda h, i: (h, 0, 0, 0)),
                  _const_spec((1, DA_QK_DIM)), _const_spec((1, DA_QK_DIM)),
                  _const_spec((1, DA_QK_DIM)), _const_spec((1, DA_QK_DIM)),
                  _const_spec((1, DA_V_DIM))],
        out_specs=pl.BlockSpec((tq, LANES), lambda h, i: (i, h)),
        out_shape=jax.ShapeDtypeStruct((S, v_w), BF16),
        scratch_shapes=([pltpu.VMEM((v_rows, tile), F32)] * n_grp
                        + [pltpu.VMEM((1, tile), F32)] * n_grp
                        + [pltpu.VMEM((tile, tile), F32)] * (2 * n_grp)),
        compiler_params=_compiler_params(2),
        name="diff_attn",
    )(qT, k_a, vT, row1(lam_q1[0]), row1(lam_k1[0]), row1(lam_q2[0]), row1(lam_k2[0]),
      row1(da_norm_g[0]))

    mrows = lambda w: pl.BlockSpec((ml_blk, w), lambda c: (c, 0))
    hm = pl.pallas_call(
        _mlstm_kernel,
        grid=(S // ml_blk,),
        in_specs=[mrows(ml_w), mrows(ml_w), mrows(ml_w), mrows(ml_w), mrows(LANES),
                  _const_spec((1, ML_DIM))],
        out_specs=mrows(ml_w),
        out_shape=jax.ShapeDtypeStruct((S, ml_w), BF16),
        scratch_shapes=[pltpu.VMEM((ML_HEADS, ML_DIM, 2 * ML_DIM), F32),
                        pltpu.VMEM((ML_HEADS, SUBLANES, LANES), F32)],
        compiler_params=_compiler_params(1),
        name="mlstm",
    )(qm, km, vm, om, gates, row1(ml_norm_g[0]))

    n_chunks = d_ff // FF_CHUNK
    wu = w_up[0]
    chunk_cols = lambda w: w.reshape(w.shape[0], n_chunks, FF_CHUNK).transpose(1, 0, 2)
    w_up_a = chunk_cols(wu[:, :d_ff]).astype(BF16)
    w_up_g = chunk_cols(wu[:, d_ff:]).astype(BF16)
    conv_w = chunk_cols(conv_f_w[0])
    conv_b = conv_f_b[0].reshape(n_chunks, 1, FF_CHUNK)
    w_dn = w_down[0].reshape(n_chunks, FF_CHUNK, D).astype(BF16)
    prow = lambda w: pl.BlockSpec((tm4, w), lambda i: (i, 0))
    consts = (w_ya[0].astype(BF16), w_yb[0].astype(BF16), w_o[0].astype(BF16), row1(norm2_g[0]),
              w_up_a, w_up_g, conv_w, conv_b, w_dn, w_ple[0].astype(BF16), w_pg[0].astype(BF16),
              row1(final_g))
    out = pl.pallas_call(
        _post_kernel,
        grid=(S // tm4,),
        in_specs=[prow(D), prow(v_w), prow(ml_w), prow(D), prow(D), prow(p.shape[-1])]
                 + [_const_spec(a.shape) for a in consts],
        out_specs=prow(D),
        out_shape=jax.ShapeDtypeStruct((S, D), x.dtype),
        scratch_shapes=[pltpu.VMEM((n_chunks, SUBLANES, FF_CHUNK), F32),
                        pltpu.VMEM((tm4 + SUBLANES, FF_CHUNK), F32)],
        compiler_params=_compiler_params(1),
        name="post",
    )(x2d, ao, hm, ga, gb, p[0, 0], *consts)
    return out[None]
```

```python
import functools
import math

import jax
import jax.numpy as jnp
from jax import lax
from jax.experimental import pallas as pl
from jax.experimental.pallas import tpu as pltpu

F32 = jnp.float32
BF16 = jnp.bfloat16

CHUNK = 64
ROPE_THETA = 10000.0
DA_HEADS = 4
DA_QK_DIM = 64
DA_V_DIM = 128
ML_HEADS = 4
ML_DIM = 128
EPS = 1e-6
LOG2E = math.log2(math.e)
V_PAD_ROWS = 16
LANES = 128
SUBLANES = 8
VMEM_LIMIT_BYTES = 56 * 1024 * 1024

PROJ_ROWS = 512
ATTN_TILE = 256
ATTN_Q_TILES = 2
ML_BLOCK = 256
POST_ROWS = 512
FF_CHUNK = 256


def _dot(a, b):
    return jnp.dot(a, b, preferred_element_type=F32)


def _sigmoid(x):
    return 1.0 / (1.0 + jnp.exp(-x))


def _rmsnorm(x, g):
    return x * lax.rsqrt(jnp.mean(x * x, axis=-1, keepdims=True) + EPS) * g


def _const_spec(shape):
    nd = len(shape)
    return pl.BlockSpec(shape, lambda *_: (0,) * nd)


def _proj_kernel(x_ref, g1_ref, cos_ref, sin_ref, w_ref, bif_ref, convw_ref,
                 qT_ref, k_ref, vT_ref, qmT_ref, km_ref, vmT_ref, om_ref, gates_ref, ga_ref, gb_ref,
                 carry_ref, cbuf_ref, *, tile):
    i = pl.program_id(0)
    tm = x_ref.shape[0]
    n_sub = tm // tile
    hb = _rmsnorm(x_ref[...], g1_ref[...]).astype(BF16)

    qk_w = 2 * DA_HEADS * 2 * DA_QK_DIM
    v_w = DA_HEADS * DA_V_DIM
    ml_w = ML_HEADS * ML_DIM
    d_model = x_ref.shape[1]
    bounds = [0]
    for width in (qk_w, v_w, 2 * ml_w, ml_w, ml_w, LANES, d_model, d_model):
        bounds.append(bounds[-1] + width)

    def proj(n):
        return _dot(hb, w_ref[:, bounds[n]:bounds[n + 1]])

    pad_row = lax.broadcasted_iota(jnp.int32, (V_PAD_ROWS, tm), 0)
    ones_rows = jnp.where(pad_row == 0, 1.0, 0.0).astype(BF16)

    def put_transposed(dst_ref, h, cols, extra_rows=None):
        cT = cols.T.astype(BF16)
        if extra_rows is not None:
            cT = jnp.concatenate([cT, extra_rows], axis=0)
        for s in range(n_sub):
            dst_ref[h, s] = cT[:, s * tile:(s + 1) * tile]

    qk = proj(0)
    cos = cos_ref[...]
    sin = sin_ref[...]
    lane = lax.broadcasted_iota(jnp.int32, (tm, LANES), 1)
    first_half = (lane % DA_QK_DIM) < (DA_QK_DIM // 2)
    n_grp = DA_HEADS
    for c in range(2 * n_grp):
        t = qk[:, c * LANES:(c + 1) * LANES]
        partner = jnp.where(first_half, pltpu.roll(t, LANES - DA_QK_DIM // 2, 1),
                            pltpu.roll(t, DA_QK_DIM // 2, 1))
        r = t * cos + partner * sin
        if c < n_grp:
            put_transposed(qT_ref, c, r * (DA_QK_DIM ** -0.5 * LOG2E))
        else:
            k_ref[:, (c - n_grp) * LANES:(c - n_grp + 1) * LANES] = r.astype(BF16)

    v = proj(1)
    for h in range(DA_HEADS):
        put_transposed(vT_ref, h, v[:, h * LANES:(h + 1) * LANES], ones_rows)

    @pl.when(i == 0)
    def _():
        carry_ref[...] = jnp.zeros_like(carry_ref)

    raw = proj(2)
    cbuf_ref[0:SUBLANES, :] = carry_ref[...]
    cbuf_ref[SUBLANES:, :] = raw
    carry_ref[...] = raw[tm - SUBLANES:, :]
    cw = convw_ref[...]
    y = (cw[3:4, :] * raw
         + cw[2:3, :] * cbuf_ref[pl.ds(SUBLANES - 1, tm), :]
         + cw[1:2, :] * cbuf_ref[pl.ds(SUBLANES - 2, tm), :]
         + cw[0:1, :] * cbuf_ref[pl.ds(SUBLANES - 3, tm), :])
    y = y * _sigmoid(y)
    half = y.shape[1] // 2
    km_ref[...] = (y[:, half:] * (ML_DIM ** -0.5)).astype(BF16)
    vm = proj(3)
    for h in range(ML_HEADS):
        put_transposed(qmT_ref, h, y[:, h * ML_DIM:(h + 1) * ML_DIM])
        put_transposed(vmT_ref, h, vm[:, h * ML_DIM:(h + 1) * ML_DIM], ones_rows)
    om_ref[...] = proj(4).astype(BF16)
    ga_ref[...] = proj(6).astype(BF16)
    gb_ref[...] = proj(7).astype(BF16)

    gp = proj(5) + bif_ref[...]
    logsig = jnp.minimum(gp, 0.0) - jnp.log1p(jnp.exp(-jnp.abs(gp)))
    gates_ref[...] = jnp.where(lane < ML_HEADS, gp, logsig)


def _attn_kernel(qT_ref, k_ref, vT_ref, lq1_ref, lk1_ref, lq2_ref, lk2_ref, g_ref, o_ref,
                 *scratch, lam_init):
    nq = ATTN_Q_TILES
    assert nq == 2
    n_grp = 2 * nq
    acc_refs = scratch[0:n_grp]
    m_refs = scratch[n_grp:2 * n_grp]
    sa_refs = scratch[2 * n_grp:3 * n_grp]
    sb_refs = scratch[3 * n_grp:4 * n_grp]
    i = pl.program_id(1)
    t = qT_ref.shape[3]
    row = lax.broadcasted_iota(jnp.int32, (qT_ref.shape[2], t), 0)
    q_grp = []
    for mp in range(2):
        keep = (row < DA_QK_DIM) if mp == 0 else (row >= DA_QK_DIM)
        for u in range(nq):
            q = qT_ref[0, u]
            q_grp.append(jnp.where(keep, q, jnp.zeros_like(q)))

    for gi in range(n_grp):
        m_refs[gi][...] = jnp.full_like(m_refs[gi], -jnp.inf)
        acc_refs[gi][...] = jnp.zeros_like(acc_refs[gi])

    def scores(j, bufs, diag):
        kt = k_ref[pl.ds(pl.multiple_of(j * t, t), t), :]
        for gi in range(n_grp):
            if diag is None or gi % nq >= diag:
                bufs[gi][...] = _dot(kt, q_grp[gi])

    def update(bufs, j, diag):
        vt = vT_ref[0, j]
        for gi in range(n_grp):
            u = gi % nq
            if diag is not None and u < diag:
                continue
            s = bufs[gi][...]
            if diag is not None and u == diag:
                kc = lax.broadcasted_iota(jnp.int32, s.shape, 0) // CHUNK
                qc = lax.broadcasted_iota(jnp.int32, s.shape, 1) // CHUNK
                s = jnp.where(kc <= qc, s, -jnp.inf)
            m_prev = m_refs[gi][...]
            m_new = jnp.maximum(m_prev, jnp.max(s, axis=0, keepdims=True))
            alpha = jnp.exp2(m_prev - m_new)
            p = jnp.exp2(s - m_new).astype(BF16)
            acc_refs[gi][...] = alpha * acc_refs[gi][...] + _dot(vt, p)
            m_refs[gi][...] = m_new

    scores(0, sa_refs, None)

    def pair(j):
        scores(j + 1, sb_refs, None)
        update(sa_refs, j, None)
        scores(j + 2, sa_refs, None)
        update(sb_refs, j + 1, None)

    def quad(g, carry):
        pair(4 * g)
        pair(4 * g + 2)
        return carry

    lax.fori_loop(0, i // 2, quad, 0)

    @pl.when(i % 2 == 1)
    def _():
        pair(2 * (i - 1))

    scores(2 * i + 1, sb_refs, 1)
    update(sa_refs, 2 * i, 0)
    update(sb_refs, 2 * i + 1, 1)

    lam = (jnp.exp(jnp.sum(lq1_ref[...] * lk1_ref[...], axis=-1, keepdims=True))
           - jnp.exp(jnp.sum(lq2_ref[...] * lk2_ref[...], axis=-1, keepdims=True))
           + lam_init)
    for u in range(nq):
        a1 = acc_refs[u][...]
        a2 = acc_refs[nq + u][...]
        ao = (a1[:DA_V_DIM, :] / a1[DA_V_DIM:DA_V_DIM + 1, :]
              - lam * (a2[:DA_V_DIM, :] / a2[DA_V_DIM:DA_V_DIM + 1, :]))
        y = ao * lax.rsqrt(jnp.mean(ao * ao, axis=0, keepdims=True) + EPS)
        o_ref[u * t:(u + 1) * t, :] = (y.T * g_ref[...] * (1.0 - lam_init)).astype(o_ref.dtype)


def _split3(x):
    hi = x.astype(BF16).astype(F32)
    r = x - hi
    mid = r.astype(BF16).astype(F32)
    lo = (r - mid).astype(BF16).astype(F32)
    return hi, mid, lo


def _mlstm_kernel(qT_ref, k_ref, vT_ref, om_ref, gates_ref, g_ref, o_ref, state_ref, m_ref):
    c = pl.program_id(0)
    L = k_ref.shape[0]

    @pl.when(c == 0)
    def _():
        state_ref[...] = jnp.zeros_like(state_ref)
        m_ref[...] = jnp.zeros_like(m_ref)

    gates = gates_ref[...]
    s_idx = lax.broadcasted_iota(jnp.int32, (L, L), 0)
    t_idx = lax.broadcasted_iota(jnp.int32, (L, L), 1)
    causal = s_idx <= t_idx
    tri = jnp.where(t_idx <= s_idx, 1.0, 0.0).astype(BF16)
    parts = _dot(tri, jnp.concatenate(_split3(gates), axis=1).astype(BF16))
    cum = parts[:, :LANES] + parts[:, LANES:2 * LANES] + parts[:, 2 * LANES:]
    gatesT = gates.T
    cumT = cum.T

    for h in range(ML_HEADS):
        sl = slice(h * ML_DIM, (h + 1) * ML_DIM)
        b_row = cumT[ML_HEADS + h:ML_HEADS + h + 1, :]
        i_row = gatesT[h:h + 1, :]
        c_col = gates[:, h:h + 1] - cum[:, ML_HEADS + h:ML_HEADS + h + 1]
        m_prev = m_ref[h, 0:1, 0:1]

        d = jnp.where(causal, c_col + b_row, -jnp.inf)
        inter = b_row + m_prev
        m_t = jnp.maximum(inter, jnp.max(d, axis=0, keepdims=True))
        w_intra = jnp.exp(d - m_t)
        w_inter = jnp.exp(inter - m_t)

        qT = qT_ref[h, 0]
        k = k_ref[:, sl]
        vT = vT_ref[h, 0]
        sT = (_dot(k, qT) * w_intra).astype(BF16)
        state = state_ref[h]
        nd = w_inter * _dot(state.astype(BF16), qT) + _dot(vT, sT)
        num = nd[:ML_DIM, :]
        den = nd[ML_DIM:ML_DIM + 1, :]
        hh = num / jnp.maximum(jnp.abs(den), jnp.exp(-m_t))
        mu = jnp.mean(hh, axis=0, keepdims=True)
        xc = hh - mu
        var = jnp.mean(xc * xc, axis=0, keepdims=True)
        y = (xc * lax.rsqrt(var + EPS)).T * g_ref[...]
        y = y * _sigmoid(om_ref[:, sl].astype(F32))
        o_ref[:, sl] = y.astype(o_ref.dtype)

        b_last = b_row[:, L - 1:L]
        g_row = b_last - b_row + i_row
        m_new = jnp.maximum(b_last + m_prev, jnp.max(g_row, axis=1, keepdims=True))
        wk = jnp.exp(g_row - m_new)
        decay = jnp.exp(b_last + m_prev - m_new)
        vTw = (vT.astype(F32) * wk).astype(BF16)
        state_ref[h] = decay * state + _dot(vTw, k)
        m_ref[h] = jnp.broadcast_to(m_new, m_ref.shape[1:])


def _gelu_tanh(x):
    return 0.5 * x * (1.0 + jnp.tanh(math.sqrt(2.0 / math.pi) * (x + 0.044715 * (x * x * x))))


def _post_kernel(x_ref, ao_ref, hm_ref, ga_ref, gb_ref, p_ref, wya_ref, wyb_ref, wo_ref, n2g_ref,
                 wup_ref, convw_ref, convb_ref, wdown_ref, wple_ref, wpg_ref, fg_ref,
                 o_ref, carry_ref, abuf_ref):
    i = pl.program_id(0)
    tm = x_ref.shape[0]
    fc = abuf_ref.shape[1]
    d_ff = wdown_ref.shape[0]
    n_chunks = d_ff // fc

    @pl.when(i == 0)
    def _():
        carry_ref[...] = jnp.zeros_like(carry_ref)

    ya = _dot(ao_ref[...], wya_ref[...])
    yb = _dot(hm_ref[...], wyb_ref[...])
    merged = _sigmoid(ga_ref[...].astype(F32)) * ya + _sigmoid(gb_ref[...].astype(F32)) * yb
    x1 = x_ref[...] + _dot(merged.astype(BF16), wo_ref[...])

    h2 = _rmsnorm(x1, n2g_ref[...]).astype(BF16)
    acc = jnp.zeros(x1.shape, F32)
    for c in range(n_chunks):
        cs = slice(c * fc, (c + 1) * fc)
        a = _dot(h2, wup_ref[:, cs])
        g = _dot(h2, wup_ref[:, d_ff + c * fc:d_ff + (c + 1) * fc])
        abuf_ref[0:SUBLANES, :] = carry_ref[c]
        abuf_ref[SUBLANES:, :] = a
        carry_ref[c] = a[tm - SUBLANES:, :]
        cw = convw_ref[:, cs]
        ac = (cw[2:3, :] * a
              + cw[1:2, :] * abuf_ref[pl.ds(SUBLANES - 1, tm), :]
              + cw[0:1, :] * abuf_ref[pl.ds(SUBLANES - 2, tm), :]
              + convb_ref[:, cs])
        u = (_gelu_tanh(ac) * g).astype(BF16)
        acc = acc + _dot(u, wdown_ref[cs, :])
    x2 = x1 + acc

    pe = _dot(p_ref[...].astype(BF16), wple_ref[...])
    pg = _dot(x2.astype(BF16), wpg_ref[...])
    x3 = x2 + pe * _sigmoid(pg)
    o_ref[...] = _rmsnorm(x3, fg_ref[...])


def _compiler_params(n_axes):
    return pltpu.CompilerParams(dimension_semantics=("arbitrary",) * n_axes,
                                vmem_limit_bytes=VMEM_LIMIT_BYTES)


def _rope_tables(S):
    half = DA_QK_DIM // 2
    inv_freq = ROPE_THETA ** (-jnp.arange(0, DA_QK_DIM, 2, dtype=F32) / DA_QK_DIM)
    ang = jnp.arange(S, dtype=F32)[:, None] * inv_freq[None, :]
    cos, sin = jnp.cos(ang), jnp.sin(ang)
    reps = LANES // half
    cos_t = jnp.tile(cos, (1, reps))
    sin_t = jnp.concatenate([-sin, sin] * (reps // 2), axis=1)
    return cos_t, sin_t


def kernel(x, p, norm1_g, w_in, b_if, conv_m_w, lam_q1, lam_k1, lam_q2, lam_k2, da_norm_g,
           ml_norm_g, w_ya, w_yb, w_o, norm2_g, w_up, conv_f_w, conv_f_b, w_down, w_ple, w_pg,
           final_g):
    B, S, D = x.shape
    depth = w_in.shape[0]
    assert B == 1 and depth == 1
    d_ff = w_down.shape[1]
    qk_w = DA_HEADS * 2 * DA_QK_DIM
    v_w = DA_HEADS * DA_V_DIM
    ml_w = ML_HEADS * ML_DIM
    lam_init = 0.8 - 0.6 * math.exp(-0.3 * 0)

    tm1 = min(PROJ_ROWS, S)
    tile = min(ATTN_TILE, S)
    ml_blk = min(ML_BLOCK, S)
    tm4 = min(POST_ROWS, S)
    assert S % tm1 == 0 and S % tile == 0 and S % tm4 == 0 and tm1 % tile == 0
    assert ml_blk == tile and DA_V_DIM == ML_DIM
    assert d_ff % FF_CHUNK == 0
    n_tiles = S // tile
    tq = ATTN_Q_TILES * tile
    n_grp = 2 * ATTN_Q_TILES
    assert S % tq == 0
    v_rows = DA_V_DIM + V_PAD_ROWS

    x2d = x[0]
    wi = w_in[0]
    if_lo = 2 * qk_w + v_w + 4 * ml_w
    if_hi = if_lo + 2 * ML_HEADS
    w_all = jnp.concatenate(
        [wi[:, :if_hi], jnp.zeros((D, LANES - 2 * ML_HEADS), wi.dtype), wi[:, if_hi:]],
        axis=1).astype(BF16)
    bif = jnp.pad(b_if[0], (0, LANES - 2 * ML_HEADS))[None, :]
    cos_t, sin_t = _rope_tables(S)

    row1 = lambda a: a.reshape(1, -1)

    proj_out_shapes = (
        jax.ShapeDtypeStruct((DA_HEADS, n_tiles, LANES, tile), BF16),
        jax.ShapeDtypeStruct((S, qk_w), BF16),
        jax.ShapeDtypeStruct((DA_HEADS, n_tiles, v_rows, tile), BF16),
        jax.ShapeDtypeStruct((ML_HEADS, n_tiles, LANES, tile), BF16),
        jax.ShapeDtypeStruct((S, ml_w), BF16),
        jax.ShapeDtypeStruct((ML_HEADS, n_tiles, v_rows, tile), BF16),
        jax.ShapeDtypeStruct((S, ml_w), BF16),
        jax.ShapeDtypeStruct((S, LANES), F32),
        jax.ShapeDtypeStruct((S, D), BF16),
        jax.ShapeDtypeStruct((S, D), BF16),
    )
    rows = lambda w: pl.BlockSpec((tm1, w), lambda i: (i, 0))
    tspec = pl.BlockSpec((DA_HEADS, tm1 // tile, LANES, tile), lambda i: (0, i, 0, 0))
    vspec = pl.BlockSpec((DA_HEADS, tm1 // tile, v_rows, tile), lambda i: (0, i, 0, 0))
    qT, k_a, vT, qmT, km, vmT, om, gates, ga, gb = pl.pallas_call(
        functools.partial(_proj_kernel, tile=tile),
        grid=(S // tm1,),
        in_specs=[rows(D), _const_spec((1, D)), rows(LANES), rows(LANES),
                  _const_spec(w_all.shape), _const_spec((1, LANES)),
                  _const_spec(conv_m_w.shape[1:])],
        out_specs=(tspec, rows(qk_w), vspec, tspec, rows(ml_w), vspec, rows(ml_w),
                   rows(LANES), rows(D), rows(D)),
        out_shape=proj_out_shapes,
        scratch_shapes=[pltpu.VMEM((SUBLANES, 2 * ml_w), F32),
                        pltpu.VMEM((tm1 + SUBLANES, 2 * ml_w), F32)],
        compiler_params=_compiler_params(1),
        name="proj",
    )(x2d, row1(norm1_g[0]), cos_t, sin_t, w_all, bif, conv_m_w[0])

    ao = pl.pallas_call(
        functools.partial(_attn_kernel, lam_init=lam_init),
        grid=(DA_HEADS, S // tq),
        in_specs=[pl.BlockSpec((1, ATTN_Q_TILES, LANES, tile), lambda h, i: (h, i, 0, 0)),
                  pl.BlockSpec((S, LANES), lambda h, i: (0, h)),
                  pl.BlockSpec((1, n_tiles, v_rows, tile), lambda h, i: (h, 0, 0, 0)),
                  _const_spec((1, DA_QK_DIM)), _const_spec((1, DA_QK_DIM)),
                  _const_spec((1, DA_QK_DIM)), _const_spec((1, DA_QK_DIM)),
                  _const_spec((1, DA_V_DIM))],
        out_specs=pl.BlockSpec((tq, LANES), lambda h, i: (i, h)),
        out_shape=jax.ShapeDtypeStruct((S, v_w), BF16),
        scratch_shapes=([pltpu.VMEM((v_rows, tile), F32)] * n_grp
                        + [pltpu.VMEM((1, tile), F32)] * n_grp
                        + [pltpu.VMEM((tile, tile), F32)] * (2 * n_grp)),
        compiler_params=_compiler_params(2),
        name="diff_attn",
    )(qT, k_a, vT, row1(lam_q1[0]), row1(lam_k1[0]), row1(lam_q2[0]), row1(lam_k2[0]),
      row1(da_norm_g[0]))

    mrows = lambda w: pl.BlockSpec((ml_blk, w), lambda c: (c, 0))
    hm = pl.pallas_call(
        _mlstm_kernel,
        grid=(S // ml_blk,),
        in_specs=[pl.BlockSpec((ML_HEADS, 1, LANES, ml_blk), lambda c: (0, c, 0, 0)),
                  mrows(ml_w),
                  pl.BlockSpec((ML_HEADS, 1, v_rows, ml_blk), lambda c: (0, c, 0, 0)),
                  mrows(ml_w), mrows(LANES), _const_spec((1, ML_DIM))],
        out_specs=mrows(ml_w),
        out_shape=jax.ShapeDtypeStruct((S, ml_w), BF16),
        scratch_shapes=[pltpu.VMEM((ML_HEADS, v_rows, ML_DIM), F32),
                        pltpu.VMEM((ML_HEADS, SUBLANES, LANES), F32)],
        compiler_params=_compiler_params(1),
        name="mlstm",
    )(qmT, km, vmT, om, gates, row1(ml_norm_g[0]))

    n_chunks = d_ff // FF_CHUNK
    conv_w = conv_f_w[0]
    conv_b = row1(conv_f_b[0])
    w_dn = w_down[0].astype(BF16)
    prow = lambda w: pl.BlockSpec((tm4, w), lambda i: (i, 0))
    consts = (w_ya[0].astype(BF16), w_yb[0].astype(BF16), w_o[0].astype(BF16), row1(norm2_g[0]),
              w_up[0].astype(BF16), conv_w, conv_b, w_dn, w_ple[0].astype(BF16), w_pg[0].astype(BF16),
              row1(final_g))
    out = pl.pallas_call(
        _post_kernel,
        grid=(S // tm4,),
        in_specs=[prow(D), prow(v_w), prow(ml_w), prow(D), prow(D), prow(p.shape[-1])]
                 + [_const_spec(a.shape) for a in consts],
        out_specs=prow(D),
        out_shape=jax.ShapeDtypeStruct((S, D), x.dtype),
        scratch_shapes=[pltpu.VMEM((n_chunks, SUBLANES, FF_CHUNK), F32),
                        pltpu.VMEM((tm4 + SUBLANES, FF_CHUNK), F32)],
        compiler_params=_compiler_params(1),
        name="post",
    )(x2d, ao, hm, ga, gb, p[0, 0], *consts)
    return out[None]
```

```python
import functools
import math

import jax
import jax.numpy as jnp
from jax import lax
from jax.experimental import pallas as pl
from jax.experimental.pallas import tpu as pltpu

F32 = jnp.float32
BF16 = jnp.bfloat16

CHUNK = 64
ROPE_THETA = 10000.0
DA_HEADS = 4
DA_QK_DIM = 64
DA_V_DIM = 128
ML_HEADS = 4
ML_DIM = 128
EPS = 1e-6
LOG2E = math.log2(math.e)
V_PAD_ROWS = 16
LANES = 128
SUBLANES = 8
VMEM_LIMIT_BYTES = 56 * 1024 * 1024

PROJ_ROWS = 512
ATTN_TILE = 256
ATTN_Q_TILES = 4
ML_BLOCK = 256
POST_ROWS = 512
FF_CHUNK = 256


def _dot(a, b):
    return jnp.dot(a, b, preferred_element_type=F32)


def _sigmoid(x):
    return 1.0 / (1.0 + jnp.exp(-x))


def _rmsnorm(x, g):
    return x * lax.rsqrt(jnp.mean(x * x, axis=-1, keepdims=True) + EPS) * g


def _const_spec(shape):
    nd = len(shape)
    return pl.BlockSpec(shape, lambda *_: (0,) * nd)


def _proj_kernel(x_ref, g1_ref, rope_base_ref, rope_cos_ref, rope_sin_ref, w_ref, wif_ref, wg_ref,
                 bif_ref, convw_ref,
                 qT_ref, k_ref, vT_ref, qmT_ref, km_ref, vmT_ref, om_ref, gates_ref, ga_ref, gb_ref,
                 carry_ref, cbuf_ref, *, tile):
    i = pl.program_id(0)
    tm = x_ref.shape[0]
    n_sub = tm // tile
    hb = _rmsnorm(x_ref[...], g1_ref[...]).astype(BF16)

    qk_w = 2 * DA_HEADS * 2 * DA_QK_DIM
    v_w = DA_HEADS * DA_V_DIM
    ml_w = ML_HEADS * ML_DIM
    d_model = x_ref.shape[1]
    bounds = [0]
    for width in (qk_w, v_w, 2 * ml_w, ml_w, ml_w):
        bounds.append(bounds[-1] + width)

    def proj(n):
        return _dot(hb, w_ref[:, bounds[n]:bounds[n + 1]])

    pad_row = lax.broadcasted_iota(jnp.int32, (V_PAD_ROWS, tm), 0)
    ones_rows = jnp.where(pad_row == 0, 1.0, 0.0).astype(BF16)

    def put_transposed(dst_ref, h, cols, extra_rows=None):
        cT = cols.astype(BF16).T
        if extra_rows is not None:
            cT = jnp.concatenate([cT, extra_rows], axis=0)
        for s in range(n_sub):
            dst_ref[h, s] = cT[:, s * tile:(s + 1) * tile]

    qk = proj(0)
    base = rope_base_ref[0]
    cos_r = rope_cos_ref[...]
    sin_r = rope_sin_ref[...]
    cos = base[0:1, :] * cos_r - base[1:2, :] * sin_r
    sin = base[3:4, :] * cos_r + base[2:3, :] * sin_r
    lane =lax.broadcasted_iota(jnp.int32, (tm, LANES), 1)
    first_half = (lane % DA_QK_DIM) < (DA_QK_DIM // 2)
    n_grp = DA_HEADS
    for c in range(2 * n_grp):
        t = qk[:, c * LANES:(c + 1) * LANES]
        partner = jnp.where(first_half, pltpu.roll(t, LANES - DA_QK_DIM // 2, 1),
                            pltpu.roll(t, DA_QK_DIM // 2, 1))
        r = t * cos + partner * sin
        if c < n_grp:
            put_transposed(qT_ref, c, r * (DA_QK_DIM ** -0.5 * LOG2E))
        else:
            k_ref[:, (c - n_grp) * LANES:(c - n_grp + 1) * LANES] = r.astype(BF16)

    v = proj(1)
    for h in range(DA_HEADS):
        put_transposed(vT_ref, h, v[:, h * LANES:(h + 1) * LANES], ones_rows)

    @pl.when(i == 0)
    def _():
        carry_ref[...] = jnp.zeros_like(carry_ref)

    raw = proj(2)
    cbuf_ref[0:SUBLANES, :] = carry_ref[...]
    cbuf_ref[SUBLANES:, :] = raw
    carry_ref[...] = raw[tm - SUBLANES:, :]
    cw = convw_ref[...]
    y = (cw[3:4, :] * raw
         + cw[2:3, :] * cbuf_ref[pl.ds(SUBLANES - 1, tm), :]
         + cw[1:2, :] * cbuf_ref[pl.ds(SUBLANES - 2, tm), :]
         + cw[0:1, :] * cbuf_ref[pl.ds(SUBLANES - 3, tm), :])
    y = y * _sigmoid(y)
    half = y.shape[1] // 2
    km_ref[...] = (y[:, half:] * (ML_DIM ** -0.5)).astype(BF16)
    vm = proj(3)
    for h in range(ML_HEADS):
        put_transposed(qmT_ref, h, y[:, h * ML_DIM:(h + 1) * ML_DIM])
        put_transposed(vmT_ref, h, vm[:, h * ML_DIM:(h + 1) * ML_DIM], ones_rows)
    om_ref[...] = proj(4).astype(BF16)
    ga_ref[...] = _dot(hb, wg_ref[:, :d_model]).astype(BF16)
    gb_ref[...] = _dot(hb, wg_ref[:, d_model:]).astype(BF16)

    gp = _dot(hb, wif_ref[...]) + bif_ref[...]
    logsig = jnp.minimum(gp, 0.0) - jnp.log1p(jnp.exp(-jnp.abs(gp)))
    gates_ref[...] = jnp.where(lane < ML_HEADS, gp, logsig)


def _attn_kernel(qT_ref, k_ref, vT_ref, lq1_ref, lk1_ref, lq2_ref, lk2_ref, g_ref, o_ref,
                 *scratch, lam_init):
    nq = ATTN_Q_TILES
    assert nq % 2 == 0
    n_grp = 2 * nq
    acc_refs = scratch[0:n_grp]
    m_refs = scratch[n_grp:2 * n_grp]
    sa_refs = scratch[2 * n_grp:3 * n_grp]
    sb_refs = scratch[3 * n_grp:4 * n_grp]
    i = pl.program_id(1)
    t = qT_ref.shape[3]
    row = lax.broadcasted_iota(jnp.int32, (qT_ref.shape[2], t), 0)
    q_grp = []
    for mp in range(2):
        keep = (row < DA_QK_DIM) if mp == 0 else (row >= DA_QK_DIM)
        for u in range(nq):
            q = qT_ref[0, u]
            q_grp.append(jnp.where(keep, q, jnp.zeros_like(q)))

    for gi in range(n_grp):
        m_refs[gi][...] = jnp.full_like(m_refs[gi], -jnp.inf)
        acc_refs[gi][...] = jnp.zeros_like(acc_refs[gi])

    def scores(j, bufs, diag):
        kt = k_ref[pl.ds(pl.multiple_of(j * t, t), t), :]
        for gi in range(n_grp):
            if diag is None or gi % nq >= diag:
                bufs[gi][...] = _dot(kt, q_grp[gi])

    def update(bufs, j, diag):
        vt = vT_ref[0, j]
        for gi in range(n_grp):
            u = gi % nq
            if diag is not None and u < diag:
                continue
            s = bufs[gi][...]
            if diag is not None and u == diag:
                kc = lax.broadcasted_iota(jnp.int32, s.shape, 0) // CHUNK
                qc = lax.broadcasted_iota(jnp.int32, s.shape, 1) // CHUNK
                s = jnp.where(kc <= qc, s, -jnp.inf)
            m_prev = m_refs[gi][...]
            m_new = jnp.maximum(m_prev, jnp.max(s, axis=0, keepdims=True))
            alpha = jnp.exp2(m_prev - m_new)
            p = jnp.exp2(s - m_new).astype(BF16)
            acc_refs[gi][...] = alpha * acc_refs[gi][...] + _dot(vt, p)
            m_refs[gi][...] = m_new

    scores(0, sa_refs, None)

    def pair(j):
        scores(j + 1, sb_refs, None)
        update(sa_refs, j, None)
        scores(j + 2, sa_refs, None)
        update(sb_refs, j + 1, None)

    n_pairs = (nq // 2) * i

    def quad(g, carry):
        pair(4 * g)
        pair(4 * g + 2)
        return carry

    lax.fori_loop(0, n_pairs // 2, quad, 0)

    if (nq // 2) % 2 == 1:
        @pl.when(n_pairs % 2 == 1)
        def _():
            pair(2 * (n_pairs - 1))

    first_diag = nq * i
    bufs = (sa_refs, sb_refs)
    for e in range(nq):
        if e + 1 < nq:
            scores(first_diag + e + 1, bufs[(e + 1) % 2], e + 1)
        update(bufs[e % 2], first_diag + e, e)

    lam = (jnp.exp(jnp.sum(lq1_ref[...] * lk1_ref[...], axis=-1, keepdims=True))
           - jnp.exp(jnp.sum(lq2_ref[...] * lk2_ref[...], axis=-1, keepdims=True))
           + lam_init)
    for u in range(nq):
        a1 = acc_refs[u][...]
        a2 = acc_refs[nq + u][...]
        ao = (a1[:DA_V_DIM, :] / a1[DA_V_DIM:DA_V_DIM + 1, :]
              - lam * (a2[:DA_V_DIM, :] / a2[DA_V_DIM:DA_V_DIM + 1, :]))
        y = ao * lax.rsqrt(jnp.mean(ao * ao, axis=0, keepdims=True) + EPS)
        o_ref[u * t:(u + 1) * t, :] = (y.T * g_ref[...] * (1.0 - lam_init)).astype(o_ref.dtype)


def _split3(x):
    hi = x.astype(BF16).astype(F32)
    r = x - hi
    mid = r.astype(BF16).astype(F32)
    lo = (r - mid).astype(BF16).astype(F32)
    return hi, mid, lo


def _mlstm_kernel(qT_ref, k_ref, vT_ref, om_ref, gates_ref, g_ref, o_ref, state_ref, m_ref):
    c = pl.program_id(0)
    L = k_ref.shape[0]

    @pl.when(c == 0)
    def _():
        state_ref[...] = jnp.zeros_like(state_ref)
        m_ref[...] = jnp.zeros_like(m_ref)

    gates = gates_ref[...]
    s_idx = lax.broadcasted_iota(jnp.int32, (L, L), 0)
    t_idx = lax.broadcasted_iota(jnp.int32, (L, L), 1)
    causal = s_idx <= t_idx
    tri = jnp.where(t_idx <= s_idx, 1.0, 0.0).astype(BF16)
    parts = _dot(tri, jnp.concatenate(_split3(gates), axis=1).astype(BF16))
    cum = parts[:, :LANES] + parts[:, LANES:2 * LANES] + parts[:, 2 * LANES:]
    gatesT = gates.T
    cumT = cum.T

    for h in range(ML_HEADS):
        sl = slice(h * ML_DIM, (h + 1) * ML_DIM)
        b_row = cumT[ML_HEADS + h:ML_HEADS + h + 1, :]
        i_row = gatesT[h:h + 1, :]
        c_col = gates[:, h:h + 1] - cum[:, ML_HEADS + h:ML_HEADS + h + 1]
        m_prev = m_ref[h, 0:1, 0:1]

        d = jnp.where(causal, c_col + b_row, -jnp.inf)
        inter = b_row + m_prev
        m_t = jnp.maximum(inter, jnp.max(d, axis=0, keepdims=True))
        w_intra = jnp.exp(d - m_t)
        w_inter = jnp.exp(inter - m_t)

        qT = qT_ref[h, 0]
        k = k_ref[:, sl]
        vT = vT_ref[h, 0]
        sT = (_dot(k, qT) * w_intra).astype(BF16)
        state = state_ref[h]
        nd = w_inter * _dot(state.astype(BF16), qT) + _dot(vT, sT)
        num = nd[:ML_DIM, :]
        den = nd[ML_DIM:ML_DIM + 1, :]
        hh = num / jnp.maximum(jnp.abs(den), jnp.exp(-m_t))
        mu = jnp.mean(hh, axis=0, keepdims=True)
        xc = hh - mu
        var = jnp.mean(xc * xc, axis=0, keepdims=True)
        y = (xc * lax.rsqrt(var + EPS)).T * g_ref[...]
        y = y * _sigmoid(om_ref[:, sl].astype(F32))
        o_ref[:, sl] = y.astype(o_ref.dtype)

        b_last = b_row[:, L - 1:L]
        g_row = b_last - b_row + i_row
        m_new = jnp.maximum(b_last + m_prev, jnp.max(g_row, axis=1, keepdims=True))
        wk = jnp.exp(g_row - m_new)
        decay = jnp.exp(b_last + m_prev - m_new)
        vTw = (vT.astype(F32) * wk).astype(BF16)
        state_ref[h] = decay * state + _dot(vTw, k)
        m_ref[h] = jnp.broadcast_to(m_new, m_ref.shape[1:])


def _gelu_tanh(x):
    return 0.5 * x * (1.0 + jnp.tanh(math.sqrt(2.0 / math.pi) * (x + 0.044715 * (x * x * x))))


def _post_kernel(x_ref, ao_ref, hm_ref, ga_ref, gb_ref, p_ref, wya_ref, wyb_ref, wo_ref, n2g_ref,
                 wup_ref, convw_ref, convb_ref, wdown_ref, wple_ref, wpg_ref, fg_ref,
                 o_ref, carry_ref, abuf_ref):
    i = pl.program_id(0)
    tm = x_ref.shape[0]
    fc = abuf_ref.shape[1]
    d_ff = wdown_ref.shape[0]
    n_chunks = d_ff // fc

    @pl.when(i == 0)
    def _():
        carry_ref[...] = jnp.zeros_like(carry_ref)

    ya = _dot(ao_ref[...], wya_ref[...])
    yb = _dot(hm_ref[...], wyb_ref[...])
    merged = _sigmoid(ga_ref[...].astype(F32)) * ya + _sigmoid(gb_ref[...].astype(F32)) * yb
    x1 = x_ref[...] + _dot(merged.astype(BF16), wo_ref[...])

    h2 = _rmsnorm(x1, n2g_ref[...]).astype(BF16)
    acc = jnp.zeros(x1.shape, F32)
    for c in range(n_chunks):
        cs = slice(c * fc, (c + 1) * fc)
        a = _dot(h2, wup_ref[:, cs])
        g = _dot(h2, wup_ref[:, d_ff + c * fc:d_ff + (c + 1) * fc])
        abuf_ref[0:SUBLANES, :] = carry_ref[c]
        abuf_ref[SUBLANES:, :] = a
        carry_ref[c] = a[tm - SUBLANES:, :]
        cw = convw_ref[:, cs]
        ac = (cw[2:3, :] * a
              + cw[1:2, :] * abuf_ref[pl.ds(SUBLANES - 1, tm), :]
              + cw[0:1, :] * abuf_ref[pl.ds(SUBLANES - 2, tm), :]
              + convb_ref[:, cs])
        u = (_gelu_tanh(ac) * g).astype(BF16)
        acc = acc + _dot(u, wdown_ref[cs, :])
    x2 = x1 + acc

    pe = _dot(p_ref[...].astype(BF16), wple_ref[...])
    pg = _dot(x2.astype(BF16), wpg_ref[...])
    x3 = x2 + pe * _sigmoid(pg)
    o_ref[...] = _rmsnorm(x3, fg_ref[...])


def _compiler_params(n_axes):
    return pltpu.CompilerParams(dimension_semantics=("arbitrary",) * n_axes,
                                vmem_limit_bytes=VMEM_LIMIT_BYTES)


def _rope_tables(S, tm):
    half = DA_QK_DIM // 2
    reps = LANES // half
    inv_freq = ROPE_THETA ** (-jnp.arange(0, DA_QK_DIM, 2, dtype=F32) / DA_QK_DIM)
    freq = jnp.tile(inv_freq, reps)[None, :]
    sign = jnp.tile(jnp.concatenate([-jnp.ones(half, F32), jnp.ones(half, F32)]), reps // 2)[None, :]
    in_tile = jnp.arange(tm, dtype=F32)[:, None] * freq
    base = (jnp.arange(S // tm, dtype=F32) * tm)[:, None] * freq
    cb, sb = jnp.cos(base), jnp.sin(base)
    zeros = jnp.zeros_like(cb)
    base_rows = jnp.stack([cb, sb, sign * cb, sign * sb] + [zeros] * (SUBLANES - 4), axis=1)
    return base_rows, jnp.cos(in_tile), jnp.sin(in_tile)


def kernel(x, p, norm1_g, w_in, b_if, conv_m_w, lam_q1, lam_k1, lam_q2, lam_k2, da_norm_g,
           ml_norm_g, w_ya, w_yb, w_o, norm2_g, w_up, conv_f_w, conv_f_b, w_down, w_ple, w_pg,
           final_g):
    B, S, D = x.shape
    depth = w_in.shape[0]
    assert B == 1 and depth == 1
    d_ff = w_down.shape[1]
    qk_w = DA_HEADS * 2 * DA_QK_DIM
    v_w = DA_HEADS * DA_V_DIM
    ml_w = ML_HEADS * ML_DIM
    lam_init = 0.8 - 0.6 * math.exp(-0.3 * 0)

    tm1 = min(PROJ_ROWS, S)
    tile = min(ATTN_TILE, S)
    ml_blk = min(ML_BLOCK, S)
    tm4 = min(POST_ROWS, S)
    assert S % tm1 == 0 and S % tile == 0 and S % tm4 == 0 and tm1 % tile == 0
    assert ml_blk == tile and DA_V_DIM == ML_DIM
    assert d_ff % FF_CHUNK == 0
    n_tiles = S // tile
    tq = ATTN_Q_TILES * tile
    n_grp = 2 * ATTN_Q_TILES
    assert S % tq == 0
    v_rows = DA_V_DIM + V_PAD_ROWS

    x2d = x[0]
    wi = w_in[0]
    if_lo = 2 * qk_w + v_w + 4 * ml_w
    if_hi = if_lo + 2 * ML_HEADS
    w_main = wi[:, :if_lo].astype(BF16)
    w_if = jnp.pad(wi[:, if_lo:if_hi], ((0, 0), (0, LANES - 2 * ML_HEADS))).astype(BF16)
    w_g = wi[:, if_hi:].astype(BF16)
    bif = jnp.pad(b_if[0], (0, LANES - 2 * ML_HEADS))[None, :]
    rope_base, rope_cos, rope_sin = _rope_tables(S, tm1)

    row1 = lambda a: a.reshape(1, -1)

    proj_out_shapes = (
        jax.ShapeDtypeStruct((DA_HEADS, n_tiles, LANES, tile), BF16),
        jax.ShapeDtypeStruct((S, qk_w), BF16),
        jax.ShapeDtypeStruct((DA_HEADS, n_tiles, v_rows, tile), BF16),
        jax.ShapeDtypeStruct((ML_HEADS, n_tiles, LANES, tile), BF16),
        jax.ShapeDtypeStruct((S, ml_w), BF16),
        jax.ShapeDtypeStruct((ML_HEADS, n_tiles, v_rows, tile), BF16),
        jax.ShapeDtypeStruct((S, ml_w), BF16),
        jax.ShapeDtypeStruct((S, LANES), F32),
        jax.ShapeDtypeStruct((S, D), BF16),
        jax.ShapeDtypeStruct((S, D), BF16),
    )
    rows = lambda w: pl.BlockSpec((tm1, w), lambda i: (i, 0))
    tspec = pl.BlockSpec((DA_HEADS, tm1 // tile, LANES, tile), lambda i: (0, i, 0, 0))
    vspec = pl.BlockSpec((DA_HEADS, tm1 // tile, v_rows, tile), lambda i: (0, i, 0, 0))
    qT, k_a, vT, qmT, km, vmT, om, gates, ga, gb = pl.pallas_call(
        functools.partial(_proj_kernel, tile=tile),
        grid=(S // tm1,),
        in_specs=[rows(D), _const_spec((1, D)),
                  pl.BlockSpec((1, SUBLANES, LANES), lambda i: (i, 0, 0)),
                  _const_spec((tm1, LANES)), _const_spec((tm1, LANES)),
                  _const_spec(w_main.shape), _const_spec(w_if.shape), _const_spec(w_g.shape),
                  _const_spec((1, LANES)), _const_spec(conv_m_w.shape[1:])],
        out_specs=(tspec, rows(qk_w), vspec, tspec, rows(ml_w), vspec, rows(ml_w),
                   rows(LANES), rows(D), rows(D)),
        out_shape=proj_out_shapes,
        scratch_shapes=[pltpu.VMEM((SUBLANES, 2 * ml_w), F32),
                        pltpu.VMEM((tm1 + SUBLANES, 2 * ml_w), F32)],
        compiler_params=_compiler_params(1),
        name="proj",
    )(x2d, row1(norm1_g[0]), rope_base, rope_cos, rope_sin, w_main, w_if, w_g, bif, conv_m_w[0])

    ao = pl.pallas_call(
        functools.partial(_attn_kernel, lam_init=lam_init),
        grid=(DA_HEADS, S // tq),
        in_specs=[pl.BlockSpec((1, ATTN_Q_TILES, LANES, tile), lambda h, i: (h, i, 0, 0)),
                  pl.BlockSpec((S, LANES), lambda h, i: (0, h)),
                  pl.BlockSpec((1, n_tiles, v_rows, tile), lambda h, i: (h, 0, 0, 0)),
                  _const_spec((1, DA_QK_DIM)), _const_spec((1, DA_QK_DIM)),
                  _const_spec((1, DA_QK_DIM)), _const_spec((1, DA_QK_DIM)),
                  _const_spec((1, DA_V_DIM))],
        out_specs=pl.BlockSpec((tq, LANES), lambda h, i: (i, h)),
        out_shape=jax.ShapeDtypeStruct((S, v_w), BF16),
        scratch_shapes=([pltpu.VMEM((v_rows, tile), F32)] * n_grp
                        + [pltpu.VMEM((1, tile), F32)] * n_grp
                        + [pltpu.VMEM((tile, tile), F32)] * (2 * n_grp)),
        compiler_params=_compiler_params(2),
        name="diff_attn",
    )(qT, k_a, vT, row1(lam_q1[0]), row1(lam_k1[0]), row1(lam_q2[0]), row1(lam_k2[0]),
      row1(da_norm_g[0]))

    mrows = lambda w: pl.BlockSpec((ml_blk, w), lambda c: (c, 0))
    hm = pl.pallas_call(
        _mlstm_kernel,
        grid=(S // ml_blk,),
        in_specs=[pl.BlockSpec((ML_HEADS, 1, LANES, ml_blk), lambda c: (0, c, 0, 0)),
                  mrows(ml_w),
                  pl.BlockSpec((ML_HEADS, 1, v_rows, ml_blk), lambda c: (0, c, 0, 0)),
                  mrows(ml_w), mrows(LANES), _const_spec((1, ML_DIM))],
        out_specs=mrows(ml_w),
        out_shape=jax.ShapeDtypeStruct((S, ml_w), BF16),
        scratch_shapes=[pltpu.VMEM((ML_HEADS, v_rows, ML_DIM), F32),
                        pltpu.VMEM((ML_HEADS, SUBLANES, LANES), F32)],
        compiler_params=_compiler_params(1),
        name="mlstm",
    )(qmT, km, vmT, om, gates, row1(ml_norm_g[0]))

    n_chunks = d_ff // FF_CHUNK
    conv_w = conv_f_w[0]
    conv_b = row1(conv_f_b[0])
    w_dn = w_down[0].astype(BF16)
    prow = lambda w: pl.BlockSpec((tm4, w), lambda i: (i, 0))
    consts = (w_ya[0].astype(BF16), w_yb[0].astype(BF16), w_o[0].astype(BF16), row1(norm2_g[0]),
              w_up[0].astype(BF16), conv_w, conv_b, w_dn, w_ple[0].astype(BF16), w_pg[0].astype(BF16),
              row1(final_g))
    out = pl.pallas_call(
        _post_kernel,
        grid=(S // tm4,),
        in_specs=[prow(D), prow(v_w), prow(ml_w), prow(D), prow(D), prow(p.shape[-1])]
                 + [_const_spec(a.shape) for a in consts],
        out_specs=prow(D),
        out_shape=jax.ShapeDtypeStruct((S, D), x.dtype),
        scratch_shapes=[pltpu.VMEM((n_chunks, SUBLANES, FF_CHUNK), F32),
                        pltpu.VMEM((tm4 + SUBLANES, FF_CHUNK), F32)],
        compiler_params=_compiler_params(1),
        name="post",
    )(x2d, ao, hm, ga, gb, p[0, 0], *consts)
    return out[None]
```

```python
import functools
import math

import jax
import jax.numpy as jnp
from jax import lax
from jax.experimental import pallas as pl
from jax.experimental.pallas import tpu as pltpu

F32 = jnp.float32
BF16 = jnp.bfloat16

CHUNK = 64
ROPE_THETA = 10000.0
DA_HEADS = 4
DA_QK_DIM = 64
DA_V_DIM = 128
ML_HEADS = 4
ML_DIM = 128
EPS = 1e-6
LOG2E = math.log2(math.e)
V_PAD_ROWS = 16
LANES = 128
SUBLANES = 8
VMEM_LIMIT_BYTES = 56 * 1024 * 1024

PROJ_ROWS = 512
ATTN_TILE = 256
ATTN_Q_TILES = 4
ML_BLOCK = 256
POST_ROWS = 512
FF_CHUNK = 256


def _dot(a, b):
    return jnp.dot(a, b, preferred_element_type=F32)


def _sigmoid(x):
    return 1.0 / (1.0 + jnp.exp(-x))


def _rmsnorm(x, g):
    return x * lax.rsqrt(jnp.mean(x * x, axis=-1, keepdims=True) + EPS) * g


def _const_spec(shape):
    nd = len(shape)
    return pl.BlockSpec(shape, lambda *_: (0,) * nd)


def _proj_kernel(x_ref, g1_ref, rope_base_ref, rope_cos_ref, rope_sin_ref, w_ref, wif_ref, wg_ref,
                 bif_ref, convw_ref,
                 qT_ref, k_ref, vT_ref, qmT_ref, km_ref, vmT_ref, om_ref, gates_ref, ga_ref, gb_ref,
                 carry_ref, cbuf_ref, *, tile):
    i = pl.program_id(0)
    tm = x_ref.shape[0]
    n_sub = tm // tile
    hb = _rmsnorm(x_ref[...], g1_ref[...]).astype(BF16)

    qk_w = 2 * DA_HEADS * 2 * DA_QK_DIM
    v_w = DA_HEADS * DA_V_DIM
    ml_w = ML_HEADS * ML_DIM
    d_model = x_ref.shape[1]
    bounds = [0]
    for width in (qk_w, v_w, 2 * ml_w, ml_w, ml_w):
        bounds.append(bounds[-1] + width)

    def proj(n):
        return _dot(hb, w_ref[:, bounds[n]:bounds[n + 1]])

    pad_row = lax.broadcasted_iota(jnp.int32, (V_PAD_ROWS, tm), 0)
    ones_rows = jnp.where(pad_row == 0, 1.0, 0.0).astype(BF16)

    def put_transposed(dst_ref, h, cols, extra_rows=None):
        cT = cols.astype(BF16).T
        if extra_rows is not None:
            cT = jnp.concatenate([cT, extra_rows], axis=0)
        for s in range(n_sub):
            dst_ref[h, s] = cT[:, s * tile:(s + 1) * tile]

    qk = proj(0)
    base = rope_base_ref[0]
    cos_r = rope_cos_ref[...]
    sin_r = rope_sin_ref[...]
    cos = base[0:1, :] * cos_r - base[1:2, :] * sin_r
    sin = base[3:4, :] * cos_r + base[2:3, :] * sin_r
    lane =lax.broadcasted_iota(jnp.int32, (tm, LANES), 1)
    first_half = (lane % DA_QK_DIM) < (DA_QK_DIM // 2)
    n_grp = DA_HEADS
    for c in range(2 * n_grp):
        t = qk[:, c * LANES:(c + 1) * LANES]
        partner = jnp.where(first_half, pltpu.roll(t, LANES - DA_QK_DIM // 2, 1),
                            pltpu.roll(t, DA_QK_DIM // 2, 1))
        r = t * cos + partner * sin
        if c < n_grp:
            put_transposed(qT_ref, c, r * (DA_QK_DIM ** -0.5 * LOG2E))
        else:
            k_ref[:, (c - n_grp) * LANES:(c - n_grp + 1) * LANES] = r.astype(BF16)

    v = proj(1)
    for h in range(DA_HEADS):
        put_transposed(vT_ref, h, v[:, h * LANES:(h + 1) * LANES], ones_rows)

    @pl.when(i == 0)
    def _():
        carry_ref[...] = jnp.zeros_like(carry_ref)

    raw = proj(2)
    cbuf_ref[0:SUBLANES, :] = carry_ref[...]
    cbuf_ref[SUBLANES:, :] = raw
    carry_ref[...] = raw[tm - SUBLANES:, :]
    cw = convw_ref[...]
    y = (cw[3:4, :] * raw
         + cw[2:3, :] * cbuf_ref[pl.ds(SUBLANES - 1, tm), :]
         + cw[1:2, :] * cbuf_ref[pl.ds(SUBLANES - 2, tm), :]
         + cw[0:1, :] * cbuf_ref[pl.ds(SUBLANES - 3, tm), :])
    y = y * _sigmoid(y)
    half = y.shape[1] // 2
    km_ref[...] = (y[:, half:] * (ML_DIM ** -0.5)).astype(BF16)
    vm = proj(3)
    for h in range(ML_HEADS):
        put_transposed(qmT_ref, h, y[:, h * ML_DIM:(h + 1) * ML_DIM])
        put_transposed(vmT_ref, h, vm[:, h * ML_DIM:(h + 1) * ML_DIM], ones_rows)
    om_ref[...] = proj(4).astype(BF16)
    ga_ref[...] = _dot(hb, wg_ref[:, :d_model]).astype(BF16)
    gb_ref[...] = _dot(hb, wg_ref[:, d_model:]).astype(BF16)

    gp = _dot(hb, wif_ref[...]) + bif_ref[...]
    logsig = jnp.minimum(gp, 0.0) - jnp.log1p(jnp.exp(-jnp.abs(gp)))
    gates_ref[...] = jnp.where(lane < ML_HEADS, gp, logsig)


def _attn_kernel(qT_ref, k_ref, vT_ref, lq1_ref, lk1_ref, lq2_ref, lk2_ref, g_ref, o_ref,
                 *scratch, lam_init):
    nq = ATTN_Q_TILES
    n_grp = 2 * nq
    acc_refs = scratch[0:n_grp]
    m_refs = scratch[n_grp:2 * n_grp]
    sa_refs = list(zip(scratch[2 * n_grp:3 * n_grp], scratch[4 * n_grp:5 * n_grp]))
    sb_refs = list(zip(scratch[3 * n_grp:4 * n_grp], scratch[5 * n_grp:6 * n_grp]))
    i = pl.program_id(1)
    t = qT_ref.shape[3]
    row = lax.broadcasted_iota(jnp.int32, (qT_ref.shape[2], t), 0)
    q_grp = []
    for mp in range(2):
        keep = (row < DA_QK_DIM) if mp == 0 else (row >= DA_QK_DIM)
        for u in range(nq):
            q = qT_ref[0, u]
            q_grp.append(jnp.where(keep, q, jnp.zeros_like(q)))

    for gi in range(n_grp):
        m_refs[gi][...] = jnp.full_like(m_refs[gi], -jnp.inf)
        acc_refs[gi][...] = jnp.zeros_like(acc_refs[gi])

    def scores(j, bufs, diag):
        kt = k_ref[pl.ds(pl.multiple_of(j * t, t), t), :]
        for gi in range(n_grp):
            if diag is None or gi % nq >= diag:
                s = _dot(kt, q_grp[gi])
                s_ref, smax_ref = bufs[gi]
                s_ref[...] = s
                smax_ref[...] = jnp.max(s, axis=0, keepdims=True)

    def update(bufs, j, diag):
        vt = vT_ref[0, j]
        for gi in range(n_grp):
            u = gi % nq
            if diag is not None and u < diag:
                continue
            s_ref, smax_ref = bufs[gi]
            s = s_ref[...]
            if diag is not None and u == diag:
                kc = lax.broadcasted_iota(jnp.int32, s.shape, 0) // CHUNK
                qc = lax.broadcasted_iota(jnp.int32, s.shape, 1) // CHUNK
                s = jnp.where(kc <= qc, s, -jnp.inf)
                s_max = jnp.max(s, axis=0, keepdims=True)
            else:
                s_max = smax_ref[...]
            m_prev = m_refs[gi][...]
            m_new = jnp.maximum(m_prev, s_max)
            alpha = jnp.exp2(m_prev - m_new)
            p = jnp.exp2(s - m_new).astype(BF16)
            acc_refs[gi][...] = alpha * acc_refs[gi][...] + _dot(vt, p)
            m_refs[gi][...] = m_new

    scores(0, sa_refs, None)

    def pair(j):
        scores(j + 1, sb_refs, None)
        update(sa_refs, j, None)
        scores(j + 2, sa_refs, None)
        update(sb_refs, j + 1, None)

    assert nq % 4 == 0

    def trip(g, carry):
        pair(4 * g)
        pair(4 * g + 2)
        return carry

    lax.fori_loop(0, (nq // 4) * i, trip, 0)

    first_diag = nq * i
    bufs = (sa_refs, sb_refs)
    for e in range(nq):
        if e + 1 < nq:
            scores(first_diag + e + 1, bufs[(e + 1) % 2], e + 1)
        update(bufs[e % 2], first_diag + e, e)

    lam = (jnp.exp(jnp.sum(lq1_ref[...] * lk1_ref[...], axis=-1, keepdims=True))
           - jnp.exp(jnp.sum(lq2_ref[...] * lk2_ref[...], axis=-1, keepdims=True))
           + lam_init)
    for u in range(nq):
        a1 = acc_refs[u][...]
        a2 = acc_refs[nq + u][...]
        ao = (a1[:DA_V_DIM, :] / a1[DA_V_DIM:DA_V_DIM + 1, :]
              - lam * (a2[:DA_V_DIM, :] / a2[DA_V_DIM:DA_V_DIM + 1, :]))
        y = ao * lax.rsqrt(jnp.mean(ao * ao, axis=0, keepdims=True) + EPS)
        o_ref[u * t:(u + 1) * t, :] = (y.T * g_ref[...] * (1.0 - lam_init)).astype(o_ref.dtype)


def _split3(x):
    hi = x.astype(BF16).astype(F32)
    r = x - hi
    mid = r.astype(BF16).astype(F32)
    lo = (r - mid).astype(BF16).astype(F32)
    return hi, mid, lo


def _mlstm_kernel(qT_ref, k_ref, vT_ref, om_ref, gates_ref, g_ref, o_ref, state_ref, m_ref):
    c = pl.program_id(0)
    L = k_ref.shape[0]

    @pl.when(c == 0)
    def _():
        state_ref[...] = jnp.zeros_like(state_ref)
        m_ref[...] = jnp.zeros_like(m_ref)

    gates = gates_ref[...]
    s_idx = lax.broadcasted_iota(jnp.int32, (L, L), 0)
    t_idx = lax.broadcasted_iota(jnp.int32, (L, L), 1)
    causal = s_idx <= t_idx
    tri = jnp.where(t_idx <= s_idx, 1.0, 0.0).astype(BF16)
    parts = _dot(tri, jnp.concatenate(_split3(gates), axis=1).astype(BF16))
    cum = parts[:, :LANES] + parts[:, LANES:2 * LANES] + parts[:, 2 * LANES:]
    gatesT = gates.T
    cumT = cum.T

    for h in range(ML_HEADS):
        sl = slice(h * ML_DIM, (h + 1) * ML_DIM)
        b_row = cumT[ML_HEADS + h:ML_HEADS + h + 1, :]
        i_row = gatesT[h:h + 1, :]
        c_col = gates[:, h:h + 1] - cum[:, ML_HEADS + h:ML_HEADS + h + 1]
        m_prev = m_ref[h, 0:1, 0:1]

        d = jnp.where(causal, c_col + b_row, -jnp.inf)
        inter = b_row + m_prev
        m_t = jnp.maximum(inter, jnp.max(d, axis=0, keepdims=True))
        w_intra = jnp.exp(d - m_t)
        w_inter = jnp.exp(inter - m_t)

        qT = qT_ref[h, 0]
        k = k_ref[:, sl]
        vT = vT_ref[h, 0]
        sT = (_dot(k, qT) * w_intra).astype(BF16)
        state = state_ref[h]
        nd = w_inter * _dot(state.astype(BF16), qT) + _dot(vT, sT)
        num = nd[:ML_DIM, :]
        den = nd[ML_DIM:ML_DIM + 1, :]
        hh = num / jnp.maximum(jnp.abs(den), jnp.exp(-m_t))
        mu = jnp.mean(hh, axis=0, keepdims=True)
        xc = hh - mu
        var = jnp.mean(xc * xc, axis=0, keepdims=True)
        y = (xc * lax.rsqrt(var + EPS)).T * g_ref[...]
        y = y * _sigmoid(om_ref[:, sl].astype(F32))
        o_ref[:, sl] = y.astype(o_ref.dtype)

        b_last = b_row[:, L - 1:L]
        g_row = b_last - b_row + i_row
        m_new = jnp.maximum(b_last + m_prev, jnp.max(g_row, axis=1, keepdims=True))
        wk = jnp.exp(g_row - m_new)
        decay = jnp.exp(b_last + m_prev - m_new)
        vTw = (vT.astype(F32) * wk).astype(BF16)
        state_ref[h] = decay * state + _dot(vTw, k)
        m_ref[h] = jnp.broadcast_to(m_new, m_ref.shape[1:])


def _gelu_tanh(x):
    return 0.5 * x * (1.0 + jnp.tanh(math.sqrt(2.0 / math.pi) * (x + 0.044715 * (x * x * x))))


def _post_kernel(x_ref, ao_ref, hm_ref, ga_ref, gb_ref, p_ref, wya_ref, wyb_ref, wo_ref, n2g_ref,
                 wup_ref, convw_ref, convb_ref, wdown_ref, wple_ref, wpg_ref, fg_ref,
                 o_ref, carry_ref, abuf_ref, u_ref):
    i = pl.program_id(0)
    tm = x_ref.shape[0]
    fc = abuf_ref.shape[1]
    d_ff = wdown_ref.shape[0]
    n_chunks = d_ff // fc

    @pl.when(i == 0)
    def _():
        carry_ref[...] = jnp.zeros_like(carry_ref)

    ya = _dot(ao_ref[...], wya_ref[...])
    yb = _dot(hm_ref[...], wyb_ref[...])
    merged = _sigmoid(ga_ref[...].astype(F32)) * ya + _sigmoid(gb_ref[...].astype(F32)) * yb
    x1 = x_ref[...] + _dot(merged.astype(BF16), wo_ref[...])

    h2 = _rmsnorm(x1, n2g_ref[...]).astype(BF16)
    def up(c):
        return (_dot(h2, wup_ref[:, c * fc:(c + 1) * fc]),
                _dot(h2, wup_ref[:, d_ff + c * fc:d_ff + (c + 1) * fc]))

    nxt = up(0)
    for c in range(n_chunks):
        cs = slice(c * fc, (c + 1) * fc)
        a, g = nxt
        if c + 1 < n_chunks:
            nxt = up(c + 1)
        abuf_ref[0:SUBLANES, :] = carry_ref[c]
        abuf_ref[SUBLANES:, :] = a
        carry_ref[c] = a[tm - SUBLANES:, :]
        cw = convw_ref[:, cs]
        ac = (cw[2:3, :] * a
              + cw[1:2, :] * abuf_ref[pl.ds(SUBLANES - 1, tm), :]
              + cw[0:1, :] * abuf_ref[pl.ds(SUBLANES - 2, tm), :]
              + convb_ref[:, cs])
        u_ref[:, cs] = (_gelu_tanh(ac) * g).astype(BF16)
    x2 = x1 + _dot(u_ref[...], wdown_ref[...])

    pe = _dot(p_ref[...].astype(BF16), wple_ref[...])
    pg = _dot(x2.astype(BF16), wpg_ref[...])
    x3 = x2 + pe * _sigmoid(pg)
    o_ref[...] = _rmsnorm(x3, fg_ref[...])


def _compiler_params(n_axes):
    return pltpu.CompilerParams(dimension_semantics=("arbitrary",) * n_axes,
                                vmem_limit_bytes=VMEM_LIMIT_BYTES)


def _rope_tables(S, tm):
    half = DA_QK_DIM // 2
    reps = LANES // half
    inv_freq = ROPE_THETA ** (-jnp.arange(0, DA_QK_DIM, 2, dtype=F32) / DA_QK_DIM)
    freq = jnp.tile(inv_freq, reps)[None, :]
    sign = jnp.tile(jnp.concatenate([-jnp.ones(half, F32), jnp.ones(half, F32)]), reps // 2)[None, :]
    in_tile = jnp.arange(tm, dtype=F32)[:, None] * freq
    base = (jnp.arange(S // tm, dtype=F32) * tm)[:, None] * freq
    cb, sb = jnp.cos(base), jnp.sin(base)
    zeros = jnp.zeros_like(cb)
    base_rows = jnp.stack([cb, sb, sign * cb, sign * sb] + [zeros] * (SUBLANES - 4), axis=1)
    return base_rows, jnp.cos(in_tile), jnp.sin(in_tile)


def kernel(x, p, norm1_g, w_in, b_if, conv_m_w, lam_q1, lam_k1, lam_q2, lam_k2, da_norm_g,
           ml_norm_g, w_ya, w_yb, w_o, norm2_g, w_up, conv_f_w, conv_f_b, w_down, w_ple, w_pg,
           final_g):
    B, S, D = x.shape
    depth = w_in.shape[0]
    assert B == 1 and depth == 1
    d_ff = w_down.shape[1]
    qk_w = DA_HEADS * 2 * DA_QK_DIM
    v_w = DA_HEADS * DA_V_DIM
    ml_w = ML_HEADS * ML_DIM
    lam_init = 0.8 - 0.6 * math.exp(-0.3 * 0)

    tm1 = min(PROJ_ROWS, S)
    tile = min(ATTN_TILE, S)
    ml_blk = min(ML_BLOCK, S)
    tm4 = min(POST_ROWS, S)
    assert S % tm1 == 0 and S % tile == 0 and S % tm4 == 0 and tm1 % tile == 0
    assert ml_blk == tile and DA_V_DIM == ML_DIM
    assert d_ff % FF_CHUNK == 0
    n_tiles = S // tile
    tq = ATTN_Q_TILES * tile
    n_grp = 2 * ATTN_Q_TILES
    assert S % tq == 0
    v_rows = DA_V_DIM + V_PAD_ROWS

    x2d = x[0]
    wi = w_in[0]
    if_lo = 2 * qk_w + v_w + 4 * ml_w
    if_hi = if_lo + 2 * ML_HEADS
    w_main = wi[:, :if_lo].astype(BF16)
    w_if = jnp.pad(wi[:, if_lo:if_hi], ((0, 0), (0, LANES - 2 * ML_HEADS))).astype(BF16)
    w_g = wi[:, if_hi:].astype(BF16)
    bif = jnp.pad(b_if[0], (0, LANES - 2 * ML_HEADS))[None, :]
    rope_base, rope_cos, rope_sin = _rope_tables(S, tm1)

    row1 = lambda a: a.reshape(1, -1)

    proj_out_shapes = (
        jax.ShapeDtypeStruct((DA_HEADS, n_tiles, LANES, tile), BF16),
        jax.ShapeDtypeStruct((S, qk_w), BF16),
        jax.ShapeDtypeStruct((DA_HEADS, n_tiles, v_rows, tile), BF16),
        jax.ShapeDtypeStruct((ML_HEADS, n_tiles, LANES, tile), BF16),
        jax.ShapeDtypeStruct((S, ml_w), BF16),
        jax.ShapeDtypeStruct((ML_HEADS, n_tiles, v_rows, tile), BF16),
        jax.ShapeDtypeStruct((S, ml_w), BF16),
        jax.ShapeDtypeStruct((S, LANES), F32),
        jax.ShapeDtypeStruct((S, D), BF16),
        jax.ShapeDtypeStruct((S, D), BF16),
    )
    rows = lambda w: pl.BlockSpec((tm1, w), lambda i: (i, 0))
    tspec = pl.BlockSpec((DA_HEADS, tm1 // tile, LANES, tile), lambda i: (0, i, 0, 0))
    vspec = pl.BlockSpec((DA_HEADS, tm1 // tile, v_rows, tile), lambda i: (0, i, 0, 0))
    qT, k_a, vT, qmT, km, vmT, om, gates, ga, gb = pl.pallas_call(
        functools.partial(_proj_kernel, tile=tile),
        grid=(S // tm1,),
        in_specs=[rows(D), _const_spec((1, D)),
                  pl.BlockSpec((1, SUBLANES, LANES), lambda i: (i, 0, 0)),
                  _const_spec((tm1, LANES)), _const_spec((tm1, LANES)),
                  _const_spec(w_main.shape), _const_spec(w_if.shape), _const_spec(w_g.shape),
                  _const_spec((1, LANES)), _const_spec(conv_m_w.shape[1:])],
        out_specs=(tspec, rows(qk_w), vspec, tspec, rows(ml_w), vspec, rows(ml_w),
                   rows(LANES), rows(D), rows(D)),
        out_shape=proj_out_shapes,
        scratch_shapes=[pltpu.VMEM((SUBLANES, 2 * ml_w), F32),
                        pltpu.VMEM((tm1 + SUBLANES, 2 * ml_w), F32)],
        compiler_params=_compiler_params(1),
        name="proj",
    )(x2d, row1(norm1_g[0]), rope_base, rope_cos, rope_sin, w_main, w_if, w_g, bif, conv_m_w[0])

    ao = pl.pallas_call(
        functools.partial(_attn_kernel, lam_init=lam_init),
        grid=(DA_HEADS, S // tq),
        in_specs=[pl.BlockSpec((1, ATTN_Q_TILES, LANES, tile), lambda h, i: (h, i, 0, 0)),
                  pl.BlockSpec((S, LANES), lambda h, i: (0, h)),
                  pl.BlockSpec((1, n_tiles, v_rows, tile), lambda h, i: (h, 0, 0, 0)),
                  _const_spec((1, DA_QK_DIM)), _const_spec((1, DA_QK_DIM)),
                  _const_spec((1, DA_QK_DIM)), _const_spec((1, DA_QK_DIM)),
                  _const_spec((1, DA_V_DIM))],
        out_specs=pl.BlockSpec((tq, LANES), lambda h, i: (i, h)),
        out_shape=jax.ShapeDtypeStruct((S, v_w), BF16),
        scratch_shapes=([pltpu.VMEM((v_rows, tile), F32)] * n_grp
                        + [pltpu.VMEM((1, tile), F32)] * n_grp
                        + [pltpu.VMEM((tile, tile), F32)] * (2 * n_grp)
                        + [pltpu.VMEM((1, tile), F32)] * (2 * n_grp)),
        compiler_params=_compiler_params(2),
        name="diff_attn",
    )(qT, k_a, vT, row1(lam_q1[0]), row1(lam_k1[0]), row1(lam_q2[0]), row1(lam_k2[0]),
      row1(da_norm_g[0]))

    mrows = lambda w: pl.BlockSpec((ml_blk, w), lambda c: (c, 0))
    hm = pl.pallas_call(
        _mlstm_kernel,
        grid=(S // ml_blk,),
        in_specs=[pl.BlockSpec((ML_HEADS, 1, LANES, ml_blk), lambda c: (0, c, 0, 0)),
                  mrows(ml_w),
                  pl.BlockSpec((ML_HEADS, 1, v_rows, ml_blk), lambda c: (0, c, 0, 0)),
                  mrows(ml_w), mrows(LANES), _const_spec((1, ML_DIM))],
        out_specs=mrows(ml_w),
        out_shape=jax.ShapeDtypeStruct((S, ml_w), BF16),
        scratch_shapes=[pltpu.VMEM((ML_HEADS, v_rows, ML_DIM), F32),
                        pltpu.VMEM((ML_HEADS, SUBLANES, LANES), F32)],
        compiler_params=_compiler_params(1),
        name="mlstm",
    )(qmT, km, vmT, om, gates, row1(ml_norm_g[0]))

    n_chunks = d_ff // FF_CHUNK
    conv_w = conv_f_w[0]
    conv_b = row1(conv_f_b[0])
    w_dn = w_down[0].astype(BF16)
    prow = lambda w: pl.BlockSpec((tm4, w), lambda i: (i, 0))
    consts = (w_ya[0].astype(BF16), w_yb[0].astype(BF16), w_o[0].astype(BF16), row1(norm2_g[0]),
              w_up[0].astype(BF16), conv_w, conv_b, w_dn, w_ple[0].astype(BF16), w_pg[0].astype(BF16),
              row1(final_g))
    out = pl.pallas_call(
        _post_kernel,
        grid=(S // tm4,),
        in_specs=[prow(D), prow(v_w), prow(ml_w), prow(D), prow(D), prow(p.shape[-1])]
                 + [_const_spec(a.shape) for a in consts],
        out_specs=prow(D),
        out_shape=jax.ShapeDtypeStruct((S, D), x.dtype),
        scratch_shapes=[pltpu.VMEM((n_chunks, SUBLANES, FF_CHUNK), F32),
                        pltpu.VMEM((tm4 + SUBLANES, FF_CHUNK), F32),
                        pltpu.VMEM((tm4, d_ff), BF16)],
        compiler_params=_compiler_params(1),
        name="post",
    )(x2d, ao, hm, ga, gb, p[0, 0], *consts)
    return out[None]
```

```python
import functools
import math

import jax
import jax.numpy as jnp
from jax import lax
from jax.experimental import pallas as pl
from jax.experimental.pallas import tpu as pltpu

F32 = jnp.float32
BF16 = jnp.bfloat16

CHUNK = 64
ROPE_THETA = 10000.0
DA_HEADS = 4
DA_QK_DIM = 64
DA_V_DIM = 128
ML_HEADS = 4
ML_DIM = 128
EPS = 1e-6
LOG2E = math.log2(math.e)
V_PAD_ROWS = 16
LANES = 128
SUBLANES = 8
VMEM_LIMIT_BYTES = 56 * 1024 * 1024

PROJ_ROWS = 512
ATTN_TILE = 256
ATTN_Q_TILES = 4
ML_BLOCK = 256
POST_ROWS = 512
FF_CHUNK = 256


def _dot(a, b):
    return jnp.dot(a, b, preferred_element_type=F32)


def _sigmoid(x):
    return 1.0 / (1.0 + jnp.exp(-x))


def _rmsnorm(x, g):
    return x * lax.rsqrt(jnp.mean(x * x, axis=-1, keepdims=True) + EPS) * g


def _const_spec(shape):
    nd = len(shape)
    return pl.BlockSpec(shape, lambda *_: (0,) * nd)


def _proj_kernel(x_ref, g1_ref, rope_base_ref, rope_cos_ref, rope_sin_ref, w_ref, wif_ref, wg_ref,
                 bif_ref, convw_ref,
                 qT_ref, k_ref, vT_ref, qmT_ref, km_ref, vmT_ref, om_ref, gates_ref, ga_ref, gb_ref,
                 carry_ref, cbuf_ref, *, tile):
    i = pl.program_id(0)
    tm = x_ref.shape[0]
    n_sub = tm // tile
    hb = _rmsnorm(x_ref[...], g1_ref[...]).astype(BF16)

    qk_w = 2 * DA_HEADS * 2 * DA_QK_DIM
    v_w = DA_HEADS * DA_V_DIM
    ml_w = ML_HEADS * ML_DIM
    d_model = x_ref.shape[1]
    bounds = [0]
    for width in (qk_w, v_w, 2 * ml_w, ml_w, ml_w):
        bounds.append(bounds[-1] + width)

    def proj(n):
        return _dot(hb, w_ref[:, bounds[n]:bounds[n + 1]])

    pad_row = lax.broadcasted_iota(jnp.int32, (V_PAD_ROWS, tm), 0)
    ones_rows = jnp.where(pad_row == 0, 1.0, 0.0).astype(BF16)

    def put_transposed(dst_ref, h, cols, extra_rows=None):
        cT = cols.astype(BF16).T
        if extra_rows is not None:
            cT = jnp.concatenate([cT, extra_rows], axis=0)
        for s in range(n_sub):
            dst_ref[h, s] = cT[:, s * tile:(s + 1) * tile]

    lane = lax.broadcasted_iota(jnp.int32, (tm, LANES), 1)
    gp = _dot(hb, wif_ref[...]) + bif_ref[...]
    logsig = jnp.minimum(gp, 0.0) - jnp.log1p(jnp.exp(-jnp.abs(gp)))
    gates_ref[...] = jnp.where(lane < ML_HEADS, gp, logsig)

    qk = proj(0)
    ga_ref[...] = _dot(hb, wg_ref[:, :d_model]).astype(BF16)
    base = rope_base_ref[0]
    cos_r = rope_cos_ref[...]
    sin_r = rope_sin_ref[...]
    cos = base[0:1, :] * cos_r - base[1:2, :] * sin_r
    sin = base[3:4, :] * cos_r + base[2:3, :] * sin_r
    first_half = (lane % DA_QK_DIM) < (DA_QK_DIM // 2)
    n_grp = DA_HEADS
    for c in range(2 * n_grp):
        t = qk[:, c * LANES:(c + 1) * LANES]
        partner = jnp.where(first_half, pltpu.roll(t, LANES - DA_QK_DIM // 2, 1),
                            pltpu.roll(t, DA_QK_DIM // 2, 1))
        r = t * cos + partner * sin
        if c < n_grp:
            put_transposed(qT_ref, c, r * (DA_QK_DIM ** -0.5 * LOG2E))
        else:
            k_ref[:, (c - n_grp) * LANES:(c - n_grp + 1) * LANES] = r.astype(BF16)

    v = proj(1)
    gb_ref[...] = _dot(hb, wg_ref[:, d_model:]).astype(BF16)
    for h in range(DA_HEADS):
        put_transposed(vT_ref, h, v[:, h * LANES:(h + 1) * LANES], ones_rows)

    @pl.when(i == 0)
    def _():
        carry_ref[...] = jnp.zeros_like(carry_ref)

    raw = proj(2)
    vm = proj(3)
    cbuf_ref[0:SUBLANES, :] = carry_ref[...]
    cbuf_ref[SUBLANES:, :] = raw
    carry_ref[...] = raw[tm - SUBLANES:, :]
    cw = convw_ref[...]
    y = (cw[3:4, :] * raw
         + cw[2:3, :] * cbuf_ref[pl.ds(SUBLANES - 1, tm), :]
         + cw[1:2, :] * cbuf_ref[pl.ds(SUBLANES - 2, tm), :]
         + cw[0:1, :] * cbuf_ref[pl.ds(SUBLANES - 3, tm), :])
    y = y * _sigmoid(y)
    half = y.shape[1] // 2
    km_ref[...] = (y[:, half:] * (ML_DIM ** -0.5)).astype(BF16)
    for h in range(ML_HEADS):
        put_transposed(qmT_ref, h, y[:, h * ML_DIM:(h + 1) * ML_DIM])
        put_transposed(vmT_ref, h, vm[:, h * ML_DIM:(h + 1) * ML_DIM], ones_rows)
    om_ref[...] = proj(4).astype(BF16)


def _attn_kernel(qT_ref, k_ref, vT_ref, lq1_ref, lk1_ref, lq2_ref, lk2_ref, g_ref, o_ref,
                 *scratch, lam_init):
    nq = ATTN_Q_TILES
    n_grp = 2 * nq
    acc_refs = scratch[0:n_grp]
    m_refs = scratch[n_grp:2 * n_grp]
    sa_refs = list(zip(scratch[2 * n_grp:3 * n_grp], scratch[4 * n_grp:5 * n_grp]))
    sb_refs = list(zip(scratch[3 * n_grp:4 * n_grp], scratch[5 * n_grp:6 * n_grp]))
    i = pl.program_id(1)
    t = qT_ref.shape[3]
    row = lax.broadcasted_iota(jnp.int32, (qT_ref.shape[2], t), 0)
    q_grp = []
    for mp in range(2):
        keep = (row < DA_QK_DIM) if mp == 0 else (row >= DA_QK_DIM)
        for u in range(nq):
            q = qT_ref[0, u]
            q_grp.append(jnp.where(keep, q, jnp.zeros_like(q)))

    for gi in range(n_grp):
        m_refs[gi][...] = jnp.full_like(m_refs[gi], -jnp.inf)
        acc_refs[gi][...] = jnp.zeros_like(acc_refs[gi])

    all_groups = tuple(range(n_grp))

    def scores(j, bufs, diag, groups=all_groups):
        kt = k_ref[pl.ds(pl.multiple_of(j * t, t), t), :]
        for gi in groups:
            if diag is None or gi % nq >= diag:
                s = _dot(kt, q_grp[gi])
                s_ref, smax_ref = bufs[gi]
                s_ref[...] = s
                smax_ref[...] = jnp.max(s, axis=0, keepdims=True)

    def update(bufs, j, diag, groups=all_groups):
        vt = vT_ref[0, j]
        for gi in groups:
            u = gi % nq
            if diag is not None and u < diag:
                continue
            s_ref, smax_ref = bufs[gi]
            s = s_ref[...]
            if diag is not None and u == diag:
                kc = lax.broadcasted_iota(jnp.int32, s.shape, 0) // CHUNK
                qc = lax.broadcasted_iota(jnp.int32, s.shape, 1) // CHUNK
                s = jnp.where(kc <= qc, s, -jnp.inf)
                s_max = jnp.max(s, axis=0, keepdims=True)
            else:
                s_max = smax_ref[...]
            m_prev = m_refs[gi][...]
            m_new = jnp.maximum(m_prev, s_max)
            alpha = jnp.exp2(m_prev - m_new)
            p = jnp.exp2(s - m_new).astype(BF16)
            acc_refs[gi][...] = alpha * acc_refs[gi][...] + _dot(vt, p)
            m_refs[gi][...] = m_new

    scores(0, sa_refs, None)

    def pair(j):
        for gi in all_groups:
            scores(j + 1, sb_refs, None, (gi,))
            update(sa_refs, j, None, (gi,))
        for gi in all_groups:
            scores(j + 2, sa_refs, None, (gi,))
            update(sb_refs, j + 1, None, (gi,))

    assert nq % 4 == 0

    def trip(g, carry):
        pair(4 * g)
        pair(4 * g + 2)
        return carry

    lax.fori_loop(0, (nq // 4) * i, trip, 0)

    first_diag = nq * i
    bufs = (sa_refs, sb_refs)
    for e in range(nq):
        for gi in all_groups:
            if e + 1 < nq:
                scores(first_diag + e + 1, bufs[(e + 1) % 2], e + 1, (gi,))
            update(bufs[e % 2], first_diag + e, e, (gi,))

    lam = (jnp.exp(jnp.sum(lq1_ref[...] * lk1_ref[...], axis=-1, keepdims=True))
           - jnp.exp(jnp.sum(lq2_ref[...] * lk2_ref[...], axis=-1, keepdims=True))
           + lam_init)
    for u in range(nq):
        a1 = acc_refs[u][...]
        a2 = acc_refs[nq + u][...]
        ao = (a1[:DA_V_DIM, :] / a1[DA_V_DIM:DA_V_DIM + 1, :]
              - lam * (a2[:DA_V_DIM, :] / a2[DA_V_DIM:DA_V_DIM + 1, :]))
        y = ao * lax.rsqrt(jnp.mean(ao * ao, axis=0, keepdims=True) + EPS)
        o_ref[u * t:(u + 1) * t, :] = (y.T * g_ref[...] * (1.0 - lam_init)).astype(o_ref.dtype)


def _split3(x):
    hi = x.astype(BF16).astype(F32)
    r = x - hi
    mid = r.astype(BF16).astype(F32)
    lo = (r - mid).astype(BF16).astype(F32)
    return hi, mid, lo


def _mlstm_kernel(qT_ref, k_ref, vT_ref, om_ref, gates_ref, g_ref, o_ref, state_ref, m_ref):
    c = pl.program_id(0)
    L = k_ref.shape[0]

    @pl.when(c == 0)
    def _():
        state_ref[...] = jnp.zeros_like(state_ref)
        m_ref[...] = jnp.zeros_like(m_ref)

    gates = gates_ref[...]
    s_idx = lax.broadcasted_iota(jnp.int32, (L, L), 0)
    t_idx = lax.broadcasted_iota(jnp.int32, (L, L), 1)
    causal = s_idx <= t_idx
    tri = jnp.where(t_idx <= s_idx, 1.0, 0.0).astype(BF16)
    parts = _dot(tri, jnp.concatenate(_split3(gates), axis=1).astype(BF16))
    cum = parts[:, :LANES] + parts[:, LANES:2 * LANES] + parts[:, 2 * LANES:]
    gatesT = gates.T
    cumT = cum.T

    heads = range(ML_HEADS)
    sl = [slice(h * ML_DIM, (h + 1) * ML_DIM) for h in heads]
    qT = [qT_ref[h, 0] for h in heads]
    vT = [vT_ref[h, 0] for h in heads]
    state = [state_ref[h] for h in heads]
    b_row = [cumT[ML_HEADS + h:ML_HEADS + h + 1, :] for h in heads]
    i_row = [gatesT[h:h + 1, :] for h in heads]
    m_prev = [m_ref[h, 0:1, 0:1] for h in heads]

    s_raw = [_dot(k_ref[:, sl[h]], qT[h]) for h in heads]
    inter_nd = [_dot(state[h].astype(BF16), qT[h]) for h in heads]

    m_t, w_inter, sT = [], [], []
    for h in heads:
        c_col = gates[:, h:h + 1] - cum[:, ML_HEADS + h:ML_HEADS + h + 1]
        d = jnp.where(causal, c_col + b_row[h], -jnp.inf)
        inter = b_row[h] + m_prev[h]
        m_t.append(jnp.maximum(inter, jnp.max(d, axis=0, keepdims=True)))
        w_inter.append(jnp.exp(inter - m_t[h]))
        sT.append((s_raw[h] * jnp.exp(d - m_t[h])).astype(BF16))

    for h in heads:
        nd = w_inter[h] * inter_nd[h] + _dot(vT[h], sT[h])
        num = nd[:ML_DIM, :]
        den = nd[ML_DIM:ML_DIM + 1, :]
        hh = num / jnp.maximum(jnp.abs(den), jnp.exp(-m_t[h]))
        mu = jnp.mean(hh, axis=0, keepdims=True)
        xc = hh - mu
        var = jnp.mean(xc * xc, axis=0, keepdims=True)
        y = (xc * lax.rsqrt(var + EPS)).T * g_ref[...]
        y = y * _sigmoid(om_ref[:, sl[h]].astype(F32))
        o_ref[:, sl[h]] = y.astype(o_ref.dtype)

    for h in heads:
        b_last = b_row[h][:, L - 1:L]
        g_row = b_last - b_row[h] + i_row[h]
        m_new = jnp.maximum(b_last + m_prev[h], jnp.max(g_row, axis=1, keepdims=True))
        wk = jnp.exp(g_row - m_new)
        decay = jnp.exp(b_last + m_prev[h] - m_new)
        vTw = (vT[h].astype(F32) * wk).astype(BF16)
        state_ref[h] = decay * state[h] + _dot(vTw, k_ref[:, sl[h]])
        m_ref[h] = jnp.broadcast_to(m_new, m_ref.shape[1:])


def _gelu_tanh(x):
    return 0.5 * x * (1.0 + jnp.tanh(math.sqrt(2.0 / math.pi) * (x + 0.044715 * (x * x * x))))


def _post_kernel(x_ref, ao_ref, hm_ref, ga_ref, gb_ref, p_ref, wya_ref, wyb_ref, wo_ref, n2g_ref,
                 wup_ref, convw_ref, convb_ref, wdown_ref, wple_ref, wpg_ref, fg_ref,
                 o_ref, carry_ref, abuf_ref, u_ref):
    i = pl.program_id(0)
    tm = x_ref.shape[0]
    fc = abuf_ref.shape[1]
    d_ff = wdown_ref.shape[0]
    n_chunks = d_ff // fc

    @pl.when(i == 0)
    def _():
        carry_ref[...] = jnp.zeros_like(carry_ref)

    ya = _dot(ao_ref[...], wya_ref[...])
    yb = _dot(hm_ref[...], wyb_ref[...])
    merged = _sigmoid(ga_ref[...].astype(F32)) * ya + _sigmoid(gb_ref[...].astype(F32)) * yb
    x1 = x_ref[...] + _dot(merged.astype(BF16), wo_ref[...])

    h2 = _rmsnorm(x1, n2g_ref[...]).astype(BF16)
    def up(c):
        return (_dot(h2, wup_ref[:, c * fc:(c + 1) * fc]),
                _dot(h2, wup_ref[:, d_ff + c * fc:d_ff + (c + 1) * fc]))

    nxt = up(0)
    for c in range(n_chunks):
        cs = slice(c * fc, (c + 1) * fc)
        a, g = nxt
        if c + 1 < n_chunks:
            nxt = up(c + 1)
        abuf_ref[0:SUBLANES, :] = carry_ref[c]
        abuf_ref[SUBLANES:, :] = a
        carry_ref[c] = a[tm - SUBLANES:, :]
        cw = convw_ref[:, cs]
        ac = (cw[2:3, :] * a
              + cw[1:2, :] * abuf_ref[pl.ds(SUBLANES - 1, tm), :]
              + cw[0:1, :] * abuf_ref[pl.ds(SUBLANES - 2, tm), :]
              + convb_ref[:, cs])
        u_ref[:, cs] = (_gelu_tanh(ac) * g).astype(BF16)
    x2 = x1 + _dot(u_ref[...], wdown_ref[...])

    pe = _dot(p_ref[...].astype(BF16), wple_ref[...])
    pg = _dot(x2.astype(BF16), wpg_ref[...])
    x3 = x2 + pe * _sigmoid(pg)
    o_ref[...] = _rmsnorm(x3, fg_ref[...])


def _compiler_params(n_axes):
    return pltpu.CompilerParams(dimension_semantics=("arbitrary",) * n_axes,
                                vmem_limit_bytes=VMEM_LIMIT_BYTES)


def _rope_tables(S, tm):
    half = DA_QK_DIM // 2
    reps = LANES // half
    inv_freq = ROPE_THETA ** (-jnp.arange(0, DA_QK_DIM, 2, dtype=F32) / DA_QK_DIM)
    freq = jnp.tile(inv_freq, reps)[None, :]
    sign = jnp.tile(jnp.concatenate([-jnp.ones(half, F32), jnp.ones(half, F32)]), reps // 2)[None, :]
    in_tile = jnp.arange(tm, dtype=F32)[:, None] * freq
    base = (jnp.arange(S // tm, dtype=F32) * tm)[:, None] * freq
    cb, sb = jnp.cos(base), jnp.sin(base)
    zeros = jnp.zeros_like(cb)
    base_rows = jnp.stack([cb, sb, sign * cb, sign * sb] + [zeros] * (SUBLANES - 4), axis=1)
    return base_rows, jnp.cos(in_tile), jnp.sin(in_tile)


def kernel(x, p, norm1_g, w_in, b_if, conv_m_w, lam_q1, lam_k1, lam_q2, lam_k2, da_norm_g,
           ml_norm_g, w_ya, w_yb, w_o, norm2_g, w_up, conv_f_w, conv_f_b, w_down, w_ple, w_pg,
           final_g):
    B, S, D = x.shape
    depth = w_in.shape[0]
    assert B == 1 and depth == 1
    d_ff = w_down.shape[1]
    qk_w = DA_HEADS * 2 * DA_QK_DIM
    v_w = DA_HEADS * DA_V_DIM
    ml_w = ML_HEADS * ML_DIM
    lam_init = 0.8 - 0.6 * math.exp(-0.3 * 0)

    tm1 = min(PROJ_ROWS, S)
    tile = min(ATTN_TILE, S)
    ml_blk = min(ML_BLOCK, S)
    tm4 = min(POST_ROWS, S)
    assert S % tm1 == 0 and S % tile == 0 and S % tm4 == 0 and tm1 % tile == 0
    assert ml_blk == tile and DA_V_DIM == ML_DIM
    assert d_ff % FF_CHUNK == 0
    n_tiles = S // tile
    tq = ATTN_Q_TILES * tile
    n_grp = 2 * ATTN_Q_TILES
    assert S % tq == 0
    v_rows = DA_V_DIM + V_PAD_ROWS

    x2d = x[0]
    wi = w_in[0]
    if_lo = 2 * qk_w + v_w + 4 * ml_w
    if_hi = if_lo + 2 * ML_HEADS
    w_main = wi[:, :if_lo].astype(BF16)
    w_if = jnp.pad(wi[:, if_lo:if_hi], ((0, 0), (0, LANES - 2 * ML_HEADS))).astype(BF16)
    w_g = wi[:, if_hi:].astype(BF16)
    bif = jnp.pad(b_if[0], (0, LANES - 2 * ML_HEADS))[None, :]
    rope_base, rope_cos, rope_sin = _rope_tables(S, tm1)

    row1 = lambda a: a.reshape(1, -1)

    proj_out_shapes = (
        jax.ShapeDtypeStruct((DA_HEADS, n_tiles, LANES, tile), BF16),
        jax.ShapeDtypeStruct((S, qk_w), BF16),
        jax.ShapeDtypeStruct((DA_HEADS, n_tiles, v_rows, tile), BF16),
        jax.ShapeDtypeStruct((ML_HEADS, n_tiles, LANES, tile), BF16),
        jax.ShapeDtypeStruct((S, ml_w), BF16),
        jax.ShapeDtypeStruct((ML_HEADS, n_tiles, v_rows, tile), BF16),
        jax.ShapeDtypeStruct((S, ml_w), BF16),
        jax.ShapeDtypeStruct((S, LANES), F32),
        jax.ShapeDtypeStruct((S, D), BF16),
        jax.ShapeDtypeStruct((S, D), BF16),
    )
    rows = lambda w: pl.BlockSpec((tm1, w), lambda i: (i, 0))
    tspec = pl.BlockSpec((DA_HEADS, tm1 // tile, LANES, tile), lambda i: (0, i, 0, 0))
    vspec = pl.BlockSpec((DA_HEADS, tm1 // tile, v_rows, tile), lambda i: (0, i, 0, 0))
    qT, k_a, vT, qmT, km, vmT, om, gates, ga, gb = pl.pallas_call(
        functools.partial(_proj_kernel, tile=tile),
        grid=(S // tm1,),
        in_specs=[rows(D), _const_spec((1, D)),
                  pl.BlockSpec((1, SUBLANES, LANES), lambda i: (i, 0, 0)),
                  _const_spec((tm1, LANES)), _const_spec((tm1, LANES)),
                  _const_spec(w_main.shape), _const_spec(w_if.shape), _const_spec(w_g.shape),
                  _const_spec((1, LANES)), _const_spec(conv_m_w.shape[1:])],
        out_specs=(tspec, rows(qk_w), vspec, tspec, rows(ml_w), vspec, rows(ml_w),
                   rows(LANES), rows(D), rows(D)),
        out_shape=proj_out_shapes,
        scratch_shapes=[pltpu.VMEM((SUBLANES, 2 * ml_w), F32),
                        pltpu.VMEM((tm1 + SUBLANES, 2 * ml_w), F32)],
        compiler_params=_compiler_params(1),
        name="proj",
    )(x2d, row1(norm1_g[0]), rope_base, rope_cos, rope_sin, w_main, w_if, w_g, bif, conv_m_w[0])

    ao = pl.pallas_call(
        functools.partial(_attn_kernel, lam_init=lam_init),
        grid=(DA_HEADS, S // tq),
        in_specs=[pl.BlockSpec((1, ATTN_Q_TILES, LANES, tile), lambda h, i: (h, i, 0, 0)),
                  pl.BlockSpec((S, LANES), lambda h, i: (0, h)),
                  pl.BlockSpec((1, n_tiles, v_rows, tile), lambda h, i: (h, 0, 0, 0)),
                  _const_spec((1, DA_QK_DIM)), _const_spec((1, DA_QK_DIM)),
                  _const_spec((1, DA_QK_DIM)), _const_spec((1, DA_QK_DIM)),
                  _const_spec((1, DA_V_DIM))],
        out_specs=pl.BlockSpec((tq, LANES), lambda h, i: (i, h)),
        out_shape=jax.ShapeDtypeStruct((S, v_w), BF16),
        scratch_shapes=([pltpu.VMEM((v_rows, tile), F32)] * n_grp
                        + [pltpu.VMEM((1, tile), F32)] * n_grp
                        + [pltpu.VMEM((tile, tile), F32)] * (2 * n_grp)
                        + [pltpu.VMEM((1, tile), F32)] * (2 * n_grp)),
        compiler_params=_compiler_params(2),
        name="diff_attn",
    )(qT, k_a, vT, row1(lam_q1[0]), row1(lam_k1[0]), row1(lam_q2[0]), row1(lam_k2[0]),
      row1(da_norm_g[0]))

    mrows = lambda w: pl.BlockSpec((ml_blk, w), lambda c: (c, 0))
    hm = pl.pallas_call(
        _mlstm_kernel,
        grid=(S // ml_blk,),
        in_specs=[pl.BlockSpec((ML_HEADS, 1, LANES, ml_blk), lambda c: (0, c, 0, 0)),
                  mrows(ml_w),
                  pl.BlockSpec((ML_HEADS, 1, v_rows, ml_blk), lambda c: (0, c, 0, 0)),
                  mrows(ml_w), mrows(LANES), _const_spec((1, ML_DIM))],
        out_specs=mrows(ml_w),
        out_shape=jax.ShapeDtypeStruct((S, ml_w), BF16),
        scratch_shapes=[pltpu.VMEM((ML_HEADS, v_rows, ML_DIM), F32),
                        pltpu.VMEM((ML_HEADS, SUBLANES, LANES), F32)],
        compiler_params=_compiler_params(1),
        name="mlstm",
    )(qmT, km, vmT, om, gates, row1(ml_norm_g[0]))

    n_chunks = d_ff // FF_CHUNK
    conv_w = conv_f_w[0]
    conv_b = row1(conv_f_b[0])
    w_dn = w_down[0].astype(BF16)
    prow = lambda w: pl.BlockSpec((tm4, w), lambda i: (i, 0))
    consts = (w_ya[0].astype(BF16), w_yb[0].astype(BF16), w_o[0].astype(BF16), row1(norm2_g[0]),
              w_up[0].astype(BF16), conv_w, conv_b, w_dn, w_ple[0].astype(BF16), w_pg[0].astype(BF16),
              row1(final_g))
    out = pl.pallas_call(
        _post_kernel,
        grid=(S // tm4,),
        in_specs=[prow(D), prow(v_w), prow(ml_w), prow(D), prow(D), prow(p.shape[-1])]
                 + [_const_spec(a.shape) for a in consts],
        out_specs=prow(D),
        out_shape=jax.ShapeDtypeStruct((S, D), x.dtype),
        scratch_shapes=[pltpu.VMEM((n_chunks, SUBLANES, FF_CHUNK), F32),
                        pltpu.VMEM((tm4 + SUBLANES, FF_CHUNK), F32),
                        pltpu.VMEM((tm4, d_ff), BF16)],
        compiler_params=_compiler_params(1),
        name="post",
    )(x2d, ao, hm, ga, gb, p[0, 0], *consts)
    return out[None]
```

```python
import functools
import math

import jax
import jax.numpy as jnp
from jax import lax
from jax.experimental import pallas as pl
from jax.experimental.pallas import tpu as pltpu

F32 = jnp.float32
BF16 = jnp.bfloat16

CHUNK = 64
ROPE_THETA = 10000.0
DA_HEADS = 4
DA_QK_DIM = 64
DA_V_DIM = 128
ML_HEADS = 4
ML_DIM = 128
EPS = 1e-6
LOG2E = math.log2(math.e)
V_PAD_ROWS = 16
LANES = 128
SUBLANES = 8
VMEM_LIMIT_BYTES = 56 * 1024 * 1024

PROJ_ROWS = 512
ATTN_TILE = 256
ATTN_Q_TILES = 4
ML_BLOCK = 256
POST_ROWS = 512
FF_CHUNK = 256


def _dot(a, b):
    return jnp.dot(a, b, preferred_element_type=F32)


def _sigmoid(x):
    return 1.0 / (1.0 + jnp.exp(-x))


def _rmsnorm(x, g):
    return x * lax.rsqrt(jnp.mean(x * x, axis=-1, keepdims=True) + EPS) * g


def _const_spec(shape):
    nd = len(shape)
    return pl.BlockSpec(shape, lambda *_: (0,) * nd)


def _proj_kernel(x_ref, g1_ref, rope_base_ref, rope_cos_ref, rope_sin_ref, w_ref, wif_ref, wg_ref,
                 bif_ref, convw_ref,
                 qT_ref, k_ref, vT_ref, qmT_ref, km_ref, vmT_ref, om_ref, gates_ref, ga_ref, gb_ref,
                 carry_ref, cbuf_ref, *, tile):
    i = pl.program_id(0)
    tm = x_ref.shape[0]
    n_sub = tm // tile
    hb = _rmsnorm(x_ref[...], g1_ref[...]).astype(BF16)

    qk_w = 2 * DA_HEADS * 2 * DA_QK_DIM
    v_w = DA_HEADS * DA_V_DIM
    ml_w = ML_HEADS * ML_DIM
    d_model = x_ref.shape[1]
    bounds = [0]
    for width in (qk_w, v_w, 2 * ml_w, ml_w, ml_w):
        bounds.append(bounds[-1] + width)

    def proj(n):
        return _dot(hb, w_ref[:, bounds[n]:bounds[n + 1]])

    pad_row = lax.broadcasted_iota(jnp.int32, (V_PAD_ROWS, tm), 0)
    ones_rows = jnp.where(pad_row == 0, 1.0, 0.0).astype(BF16)

    def put_transposed(dst_ref, h, cols, extra_rows=None):
        cT = cols.astype(BF16).T
        if extra_rows is not None:
            cT = jnp.concatenate([cT, extra_rows], axis=0)
        for s in range(n_sub):
            dst_ref[h, s] = cT[:, s * tile:(s + 1) * tile]

    lane = lax.broadcasted_iota(jnp.int32, (tm, LANES), 1)
    gp = _dot(hb, wif_ref[...]) + bif_ref[...]
    logsig = jnp.minimum(gp, 0.0) - jnp.log1p(jnp.exp(-jnp.abs(gp)))
    gates_ref[...] = jnp.where(lane < ML_HEADS, gp, logsig)

    qk = proj(0)
    ga_ref[...] = _dot(hb, wg_ref[:, :d_model]).astype(BF16)
    base = rope_base_ref[0]
    cos_r = rope_cos_ref[...]
    sin_r = rope_sin_ref[...]
    cos = base[0:1, :] * cos_r - base[1:2, :] * sin_r
    sin = base[3:4, :] * cos_r + base[2:3, :] * sin_r
    first_half = (lane % DA_QK_DIM) < (DA_QK_DIM // 2)
    n_grp = DA_HEADS
    for c in range(2 * n_grp):
        t = qk[:, c * LANES:(c + 1) * LANES]
        partner = jnp.where(first_half, pltpu.roll(t, LANES - DA_QK_DIM // 2, 1),
                            pltpu.roll(t, DA_QK_DIM // 2, 1))
        r = t * cos + partner * sin
        if c < n_grp:
            put_transposed(qT_ref, c, r * (DA_QK_DIM ** -0.5 * LOG2E))
        else:
            k_ref[:, (c - n_grp) * LANES:(c - n_grp + 1) * LANES] = r.astype(BF16)

    v = proj(1)
    gb_ref[...] = _dot(hb, wg_ref[:, d_model:]).astype(BF16)
    for h in range(DA_HEADS):
        put_transposed(vT_ref, h, v[:, h * LANES:(h + 1) * LANES], ones_rows)

    @pl.when(i == 0)
    def _():
        carry_ref[...] = jnp.zeros_like(carry_ref)

    raw = proj(2)
    vm = proj(3)
    cbuf_ref[0:SUBLANES, :] = carry_ref[...]
    cbuf_ref[SUBLANES:, :] = raw
    carry_ref[...] = raw[tm - SUBLANES:, :]
    cw = convw_ref[...]
    y = (cw[3:4, :] * raw
         + cw[2:3, :] * cbuf_ref[pl.ds(SUBLANES - 1, tm), :]
         + cw[1:2, :] * cbuf_ref[pl.ds(SUBLANES - 2, tm), :]
         + cw[0:1, :] * cbuf_ref[pl.ds(SUBLANES - 3, tm), :])
    y = y * _sigmoid(y)
    half = y.shape[1] // 2
    km_ref[...] = (y[:, half:] * (ML_DIM ** -0.5)).astype(BF16)
    for h in range(ML_HEADS):
        put_transposed(qmT_ref, h, y[:, h * ML_DIM:(h + 1) * ML_DIM])
        put_transposed(vmT_ref, h, vm[:, h * ML_DIM:(h + 1) * ML_DIM], ones_rows)
    om_ref[...] = proj(4).astype(BF16)


def _attn_kernel(qT_ref, k_ref, vT_ref, lq1_ref, lk1_ref, lq2_ref, lk2_ref, g_ref, o_ref,
                 *scratch, lam_init):
    nq = ATTN_Q_TILES
    n_grp = 2 * nq
    acc_refs = scratch[0:n_grp]
    m_refs = scratch[n_grp:2 * n_grp]
    sa_refs = list(zip(scratch[2 * n_grp:3 * n_grp], scratch[4 * n_grp:5 * n_grp]))
    sb_refs = list(zip(scratch[3 * n_grp:4 * n_grp], scratch[5 * n_grp:6 * n_grp]))
    i = pl.program_id(1)
    t = qT_ref.shape[3]
    row = lax.broadcasted_iota(jnp.int32, (qT_ref.shape[2], t), 0)
    q_grp = []
    for mp in range(2):
        keep = (row < DA_QK_DIM) if mp == 0 else (row >= DA_QK_DIM)
        for u in range(nq):
            q = qT_ref[0, u]
            q_grp.append(jnp.where(keep, q, jnp.zeros_like(q)))

    for gi in range(n_grp):
        m_refs[gi][...] = jnp.full_like(m_refs[gi], -jnp.inf)
        acc_refs[gi][...] = jnp.zeros_like(acc_refs[gi])

    all_groups = tuple(range(n_grp))

    def scores(j, bufs, diag, groups=all_groups):
        kt = k_ref[pl.ds(pl.multiple_of(j * t, t), t), :]
        for gi in groups:
            if diag is None or gi % nq >= diag:
                s = _dot(kt, q_grp[gi])
                s_ref, smax_ref = bufs[gi]
                s_ref[...] = s
                smax_ref[...] = jnp.max(s, axis=0, keepdims=True)

    def update(bufs, j, diag, groups=all_groups):
        vt = vT_ref[0, j]
        for gi in groups:
            u = gi % nq
            if diag is not None and u < diag:
                continue
            s_ref, smax_ref = bufs[gi]
            s = s_ref[...]
            if diag is not None and u == diag:
                kc = lax.broadcasted_iota(jnp.int32, s.shape, 0) // CHUNK
                qc = lax.broadcasted_iota(jnp.int32, s.shape, 1) // CHUNK
                s = jnp.where(kc <= qc, s, -jnp.inf)
                s_max = jnp.max(s, axis=0, keepdims=True)
            else:
                s_max = smax_ref[...]
            m_prev = m_refs[gi][...]
            m_new = jnp.maximum(m_prev, s_max)
            alpha = jnp.exp2(m_prev - m_new)
            p = jnp.exp2(s - m_new).astype(BF16)
            acc_refs[gi][...] = alpha * acc_refs[gi][...] + _dot(vt, p)
            m_refs[gi][...] = m_new

    scores(0, sa_refs, None)

    def pair(j):
        for gi in all_groups:
            scores(j + 1, sb_refs, None, (gi,))
            update(sa_refs, j, None, (gi,))
        for gi in all_groups:
            scores(j + 2, sa_refs, None, (gi,))
            update(sb_refs, j + 1, None, (gi,))

    assert nq == 4

    def double_pair(j):
        pair(j)
        pair(j + 2)

    def trip(g, carry):
        double_pair(8 * g)
        double_pair(8 * g + 4)
        return carry

    lax.fori_loop(0, i // 2, trip, 0)

    @pl.when(i % 2 == 1)
    def _():
        double_pair(4 * (i - 1))

    first_diag = nq * i
    bufs = (sa_refs, sb_refs)
    for e in range(nq):
        for gi in all_groups:
            if e + 1 < nq:
                scores(first_diag + e + 1, bufs[(e + 1) % 2], e + 1, (gi,))
            update(bufs[e % 2], first_diag + e, e, (gi,))

    lam = (jnp.exp(jnp.sum(lq1_ref[...] * lk1_ref[...], axis=-1, keepdims=True))
           - jnp.exp(jnp.sum(lq2_ref[...] * lk2_ref[...], axis=-1, keepdims=True))
           + lam_init)
    for u in range(nq):
        a1 = acc_refs[u][...]
        a2 = acc_refs[nq + u][...]
        ao = (a1[:DA_V_DIM, :] / a1[DA_V_DIM:DA_V_DIM + 1, :]
              - lam * (a2[:DA_V_DIM, :] / a2[DA_V_DIM:DA_V_DIM + 1, :]))
        y = ao * lax.rsqrt(jnp.mean(ao * ao, axis=0, keepdims=True) + EPS)
        o_ref[u * t:(u + 1) * t, :] = (y.T * g_ref[...] * (1.0 - lam_init)).astype(o_ref.dtype)


def _split3(x):
    hi = x.astype(BF16).astype(F32)
    r = x - hi
    mid = r.astype(BF16).astype(F32)
    lo = (r - mid).astype(BF16).astype(F32)
    return hi, mid, lo


def _mlstm_kernel(qT_ref, k_ref, vT_ref, om_ref, gates_ref, g_ref, o_ref, state_ref, m_ref):
    c = pl.program_id(0)
    L = k_ref.shape[0]

    @pl.when(c == 0)
    def _():
        state_ref[...] = jnp.zeros_like(state_ref)
        m_ref[...] = jnp.zeros_like(m_ref)

    gates = gates_ref[...]
    s_idx = lax.broadcasted_iota(jnp.int32, (L, L), 0)
    t_idx = lax.broadcasted_iota(jnp.int32, (L, L), 1)
    causal = s_idx <= t_idx
    tri = jnp.where(t_idx <= s_idx, 1.0, 0.0).astype(BF16)
    parts = _dot(tri, jnp.concatenate(_split3(gates), axis=1).astype(BF16))
    cum = parts[:, :LANES] + parts[:, LANES:2 * LANES] + parts[:, 2 * LANES:]
    gatesT = gates.T
    cumT = cum.T

    heads = range(ML_HEADS)
    sl = [slice(h * ML_DIM, (h + 1) * ML_DIM) for h in heads]
    qT = [qT_ref[h, 0] for h in heads]
    vT = [vT_ref[h, 0] for h in heads]
    state = [state_ref[h] for h in heads]
    b_row = [cumT[ML_HEADS + h:ML_HEADS + h + 1, :] for h in heads]
    i_row = [gatesT[h:h + 1, :] for h in heads]
    m_prev = [m_ref[h, 0:1, 0:1] for h in heads]

    s_raw = [_dot(k_ref[:, sl[h]], qT[h]) for h in heads]
    inter_nd = [_dot(state[h].astype(BF16), qT[h]) for h in heads]

    m_t, w_inter, sT = [], [], []
    for h in heads:
        c_col = gates[:, h:h + 1] - cum[:, ML_HEADS + h:ML_HEADS + h + 1]
        d = jnp.where(causal, c_col + b_row[h], -jnp.inf)
        inter = b_row[h] + m_prev[h]
        m_t.append(jnp.maximum(inter, jnp.max(d, axis=0, keepdims=True)))
        w_inter.append(jnp.exp(inter - m_t[h]))
        sT.append((s_raw[h] * jnp.exp(d - m_t[h])).astype(BF16))

    for h in heads:
        nd = w_inter[h] * inter_nd[h] + _dot(vT[h], sT[h])
        num = nd[:ML_DIM, :]
        den = nd[ML_DIM:ML_DIM + 1, :]
        hh = num / jnp.maximum(jnp.abs(den), jnp.exp(-m_t[h]))
        mu = jnp.mean(hh, axis=0, keepdims=True)
        xc = hh - mu
        var = jnp.mean(xc * xc, axis=0, keepdims=True)
        y = (xc * lax.rsqrt(var + EPS)).T * g_ref[...]
        y = y * _sigmoid(om_ref[:, sl[h]].astype(F32))
        o_ref[:, sl[h]] = y.astype(o_ref.dtype)

    for h in heads:
        b_last = b_row[h][:, L - 1:L]
        g_row = b_last - b_row[h] + i_row[h]
        m_new = jnp.maximum(b_last + m_prev[h], jnp.max(g_row, axis=1, keepdims=True))
        wk = jnp.exp(g_row - m_new)
        decay = jnp.exp(b_last + m_prev[h] - m_new)
        vTw = (vT[h].astype(F32) * wk).astype(BF16)
        state_ref[h] = decay * state[h] + _dot(vTw, k_ref[:, sl[h]])
        m_ref[h] = jnp.broadcast_to(m_new, m_ref.shape[1:])


def _gelu_tanh(x):
    return 0.5 * x * (1.0 + jnp.tanh(math.sqrt(2.0 / math.pi) * (x + 0.044715 * (x * x * x))))


def _post_kernel(x_ref, ao_ref, hm_ref, ga_ref, gb_ref, p_ref, wya_ref, wyb_ref, wo_ref, n2g_ref,
                 wup_ref, convw_ref, convb_ref, wdown_ref, wple_ref, wpg_ref, fg_ref,
                 o_ref, carry_ref, abuf_ref, u_ref):
    i = pl.program_id(0)
    tm = x_ref.shape[0]
    fc = abuf_ref.shape[1]
    d_ff = wdown_ref.shape[0]
    n_chunks = d_ff // fc

    @pl.when(i == 0)
    def _():
        carry_ref[...] = jnp.zeros_like(carry_ref)

    ya = _dot(ao_ref[...], wya_ref[...])
    yb = _dot(hm_ref[...], wyb_ref[...])
    merged = _sigmoid(ga_ref[...].astype(F32)) * ya + _sigmoid(gb_ref[...].astype(F32)) * yb
    x1 = x_ref[...] + _dot(merged.astype(BF16), wo_ref[...])

    h2 = _rmsnorm(x1, n2g_ref[...]).astype(BF16)
    def up(c):
        return (_dot(h2, wup_ref[:, c * fc:(c + 1) * fc]),
                _dot(h2, wup_ref[:, d_ff + c * fc:d_ff + (c + 1) * fc]))

    nxt = up(0)
    for c in range(n_chunks):
        cs = slice(c * fc, (c + 1) * fc)
        a, g = nxt
        if c + 1 < n_chunks:
            nxt = up(c + 1)
        abuf_ref[0:SUBLANES, :] = carry_ref[c]
        abuf_ref[SUBLANES:, :] = a
        carry_ref[c] = a[tm - SUBLANES:, :]
        cw = convw_ref[:, cs]
        ac = (cw[2:3, :] * a
              + cw[1:2, :] * abuf_ref[pl.ds(SUBLANES - 1, tm), :]
              + cw[0:1, :] * abuf_ref[pl.ds(SUBLANES - 2, tm), :]
              + convb_ref[:, cs])
        u_ref[:, cs] = (_gelu_tanh(ac) * g).astype(BF16)
    x2 = x1 + _dot(u_ref[...], wdown_ref[...])

    pe = _dot(p_ref[...].astype(BF16), wple_ref[...])
    pg = _dot(x2.astype(BF16), wpg_ref[...])
    x3 = x2 + pe * _sigmoid(pg)
    o_ref[...] = _rmsnorm(x3, fg_ref[...])


def _compiler_params(n_axes):
    return pltpu.CompilerParams(dimension_semantics=("arbitrary",) * n_axes,
                                vmem_limit_bytes=VMEM_LIMIT_BYTES)


def _rope_tables(S, tm):
    half = DA_QK_DIM // 2
    reps = LANES // half
    inv_freq = ROPE_THETA ** (-jnp.arange(0, DA_QK_DIM, 2, dtype=F32) / DA_QK_DIM)
    freq = jnp.tile(inv_freq, reps)[None, :]
    sign = jnp.tile(jnp.concatenate([-jnp.ones(half, F32), jnp.ones(half, F32)]), reps // 2)[None, :]
    in_tile = jnp.arange(tm, dtype=F32)[:, None] * freq
    base = (jnp.arange(S // tm, dtype=F32) * tm)[:, None] * freq
    cb, sb = jnp.cos(base), jnp.sin(base)
    zeros = jnp.zeros_like(cb)
    base_rows = jnp.stack([cb, sb, sign * cb, sign * sb] + [zeros] * (SUBLANES - 4), axis=1)
    return base_rows, jnp.cos(in_tile), jnp.sin(in_tile)


def kernel(x, p, norm1_g, w_in, b_if, conv_m_w, lam_q1, lam_k1, lam_q2, lam_k2, da_norm_g,
           ml_norm_g, w_ya, w_yb, w_o, norm2_g, w_up, conv_f_w, conv_f_b, w_down, w_ple, w_pg,
           final_g):
    B, S, D = x.shape
    depth = w_in.shape[0]
    assert B == 1 and depth == 1
    d_ff = w_down.shape[1]
    qk_w = DA_HEADS * 2 * DA_QK_DIM
    v_w = DA_HEADS * DA_V_DIM
    ml_w = ML_HEADS * ML_DIM
    lam_init = 0.8 - 0.6 * math.exp(-0.3 * 0)

    tm1 = min(PROJ_ROWS, S)
    tile = min(ATTN_TILE, S)
    ml_blk = min(ML_BLOCK, S)
    tm4 = min(POST_ROWS, S)
    assert S % tm1 == 0 and S % tile == 0 and S % tm4 == 0 and tm1 % tile == 0
    assert ml_blk == tile and DA_V_DIM == ML_DIM
    assert d_ff % FF_CHUNK == 0
    n_tiles = S // tile
    tq = ATTN_Q_TILES * tile
    n_grp = 2 * ATTN_Q_TILES
    assert S % tq == 0
    v_rows = DA_V_DIM + V_PAD_ROWS

    x2d = x[0]
    wi = w_in[0]
    if_lo = 2 * qk_w + v_w + 4 * ml_w
    if_hi = if_lo + 2 * ML_HEADS
    w_main = wi[:, :if_lo].astype(BF16)
    w_if = jnp.pad(wi[:, if_lo:if_hi], ((0, 0), (0, LANES - 2 * ML_HEADS))).astype(BF16)
    w_g = wi[:, if_hi:].astype(BF16)
    bif = jnp.pad(b_if[0], (0, LANES - 2 * ML_HEADS))[None, :]
    rope_base, rope_cos, rope_sin = _rope_tables(S, tm1)

    row1 = lambda a: a.reshape(1, -1)

    proj_out_shapes = (
        jax.ShapeDtypeStruct((DA_HEADS, n_tiles, LANES, tile), BF16),
        jax.ShapeDtypeStruct((S, qk_w), BF16),
        jax.ShapeDtypeStruct((DA_HEADS, n_tiles, v_rows, tile), BF16),
        jax.ShapeDtypeStruct((ML_HEADS, n_tiles, LANES, tile), BF16),
        jax.ShapeDtypeStruct((S, ml_w), BF16),
        jax.ShapeDtypeStruct((ML_HEADS, n_tiles, v_rows, tile), BF16),
        jax.ShapeDtypeStruct((S, ml_w), BF16),
        jax.ShapeDtypeStruct((S, LANES), F32),
        jax.ShapeDtypeStruct((S, D), BF16),
        jax.ShapeDtypeStruct((S, D), BF16),
    )
    rows = lambda w: pl.BlockSpec((tm1, w), lambda i: (i, 0))
    tspec = pl.BlockSpec((DA_HEADS, tm1 // tile, LANES, tile), lambda i: (0, i, 0, 0))
    vspec = pl.BlockSpec((DA_HEADS, tm1 // tile, v_rows, tile), lambda i: (0, i, 0, 0))
    qT, k_a, vT, qmT, km, vmT, om, gates, ga, gb = pl.pallas_call(
        functools.partial(_proj_kernel, tile=tile),
        grid=(S // tm1,),
        in_specs=[rows(D), _const_spec((1, D)),
                  pl.BlockSpec((1, SUBLANES, LANES), lambda i: (i, 0, 0)),
                  _const_spec((tm1, LANES)), _const_spec((tm1, LANES)),
                  _const_spec(w_main.shape), _const_spec(w_if.shape), _const_spec(w_g.shape),
                  _const_spec((1, LANES)), _const_spec(conv_m_w.shape[1:])],
        out_specs=(tspec, rows(qk_w), vspec, tspec, rows(ml_w), vspec, rows(ml_w),
                   rows(LANES), rows(D), rows(D)),
        out_shape=proj_out_shapes,
        scratch_shapes=[pltpu.VMEM((SUBLANES, 2 * ml_w), F32),
                        pltpu.VMEM((tm1 + SUBLANES, 2 * ml_w), F32)],
        compiler_params=_compiler_params(1),
        name="proj",
    )(x2d, row1(norm1_g[0]), rope_base, rope_cos, rope_sin, w_main, w_if, w_g, bif, conv_m_w[0])

    ao = pl.pallas_call(
        functools.partial(_attn_kernel, lam_init=lam_init),
        grid=(DA_HEADS, S // tq),
        in_specs=[pl.BlockSpec((1, ATTN_Q_TILES, LANES, tile), lambda h, i: (h, i, 0, 0)),
                  pl.BlockSpec((S, LANES), lambda h, i: (0, h)),
                  pl.BlockSpec((1, n_tiles, v_rows, tile), lambda h, i: (h, 0, 0, 0)),
                  _const_spec((1, DA_QK_DIM)), _const_spec((1, DA_QK_DIM)),
                  _const_spec((1, DA_QK_DIM)), _const_spec((1, DA_QK_DIM)),
                  _const_spec((1, DA_V_DIM))],
        out_specs=pl.BlockSpec((tq, LANES), lambda h, i: (i, h)),
        out_shape=jax.ShapeDtypeStruct((S, v_w), BF16),
        scratch_shapes=([pltpu.VMEM((v_rows, tile), F32)] * n_grp
                        + [pltpu.VMEM((1, tile), F32)] * n_grp
                        + [pltpu.VMEM((tile, tile), F32)] * (2 * n_grp)
                        + [pltpu.VMEM((1, tile), F32)] * (2 * n_grp)),
        compiler_params=_compiler_params(2),
        name="diff_attn",
    )(qT, k_a, vT, row1(lam_q1[0]), row1(lam_k1[0]), row1(lam_q2[0]), row1(lam_k2[0]),
      row1(da_norm_g[0]))

    mrows = lambda w: pl.BlockSpec((ml_blk, w), lambda c: (c, 0))
    hm = pl.pallas_call(
        _mlstm_kernel,
        grid=(S // ml_blk,),
        in_specs=[pl.BlockSpec((ML_HEADS, 1, LANES, ml_blk), lambda c: (0, c, 0, 0)),
                  mrows(ml_w),
                  pl.BlockSpec((ML_HEADS, 1, v_rows, ml_blk), lambda c: (0, c, 0, 0)),
                  mrows(ml_w), mrows(LANES), _const_spec((1, ML_DIM))],
        out_specs=mrows(ml_w),
        out_shape=jax.ShapeDtypeStruct((S, ml_w), BF16),
        scratch_shapes=[pltpu.VMEM((ML_HEADS, v_rows, ML_DIM), F32),
                        pltpu.VMEM((ML_HEADS, SUBLANES, LANES), F32)],
        compiler_params=_compiler_params(1),
        name="mlstm",
    )(qmT, km, vmT, om, gates, row1(ml_norm_g[0]))

    n_chunks = d_ff // FF_CHUNK
    conv_w = conv_f_w[0]
    conv_b = row1(conv_f_b[0])
    w_dn = w_down[0].astype(BF16)
    prow = lambda w: pl.BlockSpec((tm4, w), lambda i: (i, 0))
    consts = (w_ya[0].astype(BF16), w_yb[0].astype(BF16), w_o[0].astype(BF16), row1(norm2_g[0]),
              w_up[0].astype(BF16), conv_w, conv_b, w_dn, w_ple[0].astype(BF16), w_pg[0].astype(BF16),
              row1(final_g))
    out = pl.pallas_call(
        _post_kernel,
        grid=(S // tm4,),
        in_specs=[prow(D), prow(v_w), prow(ml_w), prow(D), prow(D), prow(p.shape[-1])]
                 + [_const_spec(a.shape) for a in consts],
        out_specs=prow(D),
        out_shape=jax.ShapeDtypeStruct((S, D), x.dtype),
        scratch_shapes=[pltpu.VMEM((n_chunks, SUBLANES, FF_CHUNK), F32),
                        pltpu.VMEM((tm4 + SUBLANES, FF_CHUNK), F32),
                        pltpu.VMEM((tm4, d_ff), BF16)],
        compiler_params=_compiler_params(1),
        name="post",
    )(x2d, ao, hm, ga, gb, p[0, 0], *consts)
    return out[None]
```

```python
import functools
import math

import jax
import jax.numpy as jnp
from jax import lax
from jax.experimental import pallas as pl
from jax.experimental.pallas import tpu as pltpu

F32 = jnp.float32
BF16 = jnp.bfloat16

CHUNK = 64
ROPE_THETA = 10000.0
DA_HEADS = 4
DA_QK_DIM = 64
DA_V_DIM = 128
ML_HEADS = 4
ML_DIM = 128
EPS = 1e-6
LOG2E = math.log2(math.e)
V_PAD_ROWS = 16
LANES = 128
SUBLANES = 8
VMEM_LIMIT_BYTES = 56 * 1024 * 1024

PROJ_ROWS = 512
ATTN_TILE = 256
ATTN_Q_TILES = 4
ML_BLOCK = 256
POST_ROWS = 512
FF_CHUNK = 256


def _dot(a, b):
    return jnp.dot(a, b, preferred_element_type=F32)


def _sigmoid(x):
    return 1.0 / (1.0 + jnp.exp(-x))


def _rmsnorm(x, g):
    return x * lax.rsqrt(jnp.mean(x * x, axis=-1, keepdims=True) + EPS) * g


def _const_spec(shape):
    nd = len(shape)
    return pl.BlockSpec(shape, lambda *_: (0,) * nd)


def _proj_kernel(x_ref, g1_ref, rope_base_ref, rope_cos_ref, rope_sin_ref, w_ref, wif_ref, wg_ref,
                 bif_ref, convw_ref, mlg_ref,
                 qT_ref, k_ref, vT_ref, hm_ref, ga_ref, gb_ref,
                 carry_ref, cbuf_ref, state_ref, m_ref, *, tile):
    i = pl.program_id(0)
    tm = x_ref.shape[0]
    n_sub = tm // tile
    hb = _rmsnorm(x_ref[...], g1_ref[...]).astype(BF16)

    qk_w = 2 * DA_HEADS * 2 * DA_QK_DIM
    v_w = DA_HEADS * DA_V_DIM
    ml_w = ML_HEADS * ML_DIM
    d_model = x_ref.shape[1]
    bounds = [0]
    for width in (qk_w, v_w, 2 * ml_w, ml_w, ml_w):
        bounds.append(bounds[-1] + width)

    def proj(n):
        return _dot(hb, w_ref[:, bounds[n]:bounds[n + 1]])

    pad_row = lax.broadcasted_iota(jnp.int32, (V_PAD_ROWS, tm), 0)
    ones_rows = jnp.where(pad_row == 0, 1.0, 0.0).astype(BF16)

    def transposed(cols, extra_rows=None):
        cT = cols.astype(BF16).T
        return cT if extra_rows is None else jnp.concatenate([cT, extra_rows], axis=0)

    def put_transposed(dst_ref, h, cols, extra_rows=None):
        cT = transposed(cols, extra_rows)
        for s in range(n_sub):
            dst_ref[h, s] = cT[:, s * tile:(s + 1) * tile]

    lane = lax.broadcasted_iota(jnp.int32, (tm, LANES), 1)
    gp = _dot(hb, wif_ref[...]) + bif_ref[...]
    logsig = jnp.minimum(gp, 0.0) - jnp.log1p(jnp.exp(-jnp.abs(gp)))
    gates = jnp.where(lane < ML_HEADS, gp, logsig)

    qk = proj(0)
    base = rope_base_ref[0]
    cos_r = rope_cos_ref[...]
    sin_r = rope_sin_ref[...]
    cos = base[0:1, :] * cos_r - base[1:2, :] * sin_r
    sin = base[3:4, :] * cos_r + base[2:3, :] * sin_r
    first_half = (lane % DA_QK_DIM) < (DA_QK_DIM // 2)
    n_grp = DA_HEADS
    for c in range(2 * n_grp):
        t = qk[:, c * LANES:(c + 1) * LANES]
        partner = jnp.where(first_half, pltpu.roll(t, LANES - DA_QK_DIM // 2, 1),
                            pltpu.roll(t, DA_QK_DIM // 2, 1))
        r = t * cos + partner * sin
        if c < n_grp:
            put_transposed(qT_ref, c, r * (DA_QK_DIM ** -0.5 * LOG2E))
        else:
            k_ref[:, (c - n_grp) * LANES:(c - n_grp + 1) * LANES] = r.astype(BF16)

    v = proj(1)
    for h in range(DA_HEADS):
        put_transposed(vT_ref, h, v[:, h * LANES:(h + 1) * LANES], ones_rows)

    @pl.when(i == 0)
    def _():
        carry_ref[...] = jnp.zeros_like(carry_ref)
        state_ref[...] = jnp.zeros_like(state_ref)
        m_ref[...] = jnp.zeros_like(m_ref)

    raw = proj(2)
    vm = proj(3)
    cbuf_ref[0:SUBLANES, :] = carry_ref[...]
    cbuf_ref[SUBLANES:, :] = raw
    carry_ref[...] = raw[tm - SUBLANES:, :]
    cw = convw_ref[...]
    y = (cw[3:4, :] * raw
         + cw[2:3, :] * cbuf_ref[pl.ds(SUBLANES - 1, tm), :]
         + cw[1:2, :] * cbuf_ref[pl.ds(SUBLANES - 2, tm), :]
         + cw[0:1, :] * cbuf_ref[pl.ds(SUBLANES - 3, tm), :])
    y = y * _sigmoid(y)
    om = proj(4)

    km = (y[:, ml_w:] * (ML_DIM ** -0.5)).astype(BF16)
    qmT = [transposed(y[:, h * ML_DIM:(h + 1) * ML_DIM]) for h in range(ML_HEADS)]
    vmT = [transposed(vm[:, h * ML_DIM:(h + 1) * ML_DIM], ones_rows) for h in range(ML_HEADS)]
    for s in range(n_sub):
        if s == 0:
            ga_ref[...] = _dot(hb, wg_ref[:, :d_model]).astype(BF16)
        if s == n_sub - 1:
            gb_ref[...] = _dot(hb, wg_ref[:, d_model:]).astype(BF16)
        rs = slice(s * tile, (s + 1) * tile)
        outs = _mlstm_block([q[:, rs] for q in qmT], km[rs, :], [v[:, rs] for v in vmT],
                            om[rs, :], gates[rs, :], mlg_ref[...], state_ref, m_ref)
        for h in range(ML_HEADS):
            hm_ref[rs, h * ML_DIM:(h + 1) * ML_DIM] = outs[h].astype(hm_ref.dtype)


def _attn_kernel(qT_ref, k_ref, vT_ref, lq1_ref, lk1_ref, lq2_ref, lk2_ref, g_ref, o_ref,
                 *scratch, lam_init):
    nq = ATTN_Q_TILES
    n_grp = 2 * nq
    acc_refs = scratch[0:n_grp]
    m_refs = scratch[n_grp:2 * n_grp]
    sa_refs = list(zip(scratch[2 * n_grp:3 * n_grp], scratch[4 * n_grp:5 * n_grp]))
    sb_refs = list(zip(scratch[3 * n_grp:4 * n_grp], scratch[5 * n_grp:6 * n_grp]))
    i = pl.program_id(1)
    t = qT_ref.shape[3]
    row = lax.broadcasted_iota(jnp.int32, (qT_ref.shape[2], t), 0)
    q_grp = []
    for mp in range(2):
        keep = (row < DA_QK_DIM) if mp == 0 else (row >= DA_QK_DIM)
        for u in range(nq):
            q = qT_ref[0, u]
            q_grp.append(jnp.where(keep, q, jnp.zeros_like(q)))

    for gi in range(n_grp):
        m_refs[gi][...] = jnp.full_like(m_refs[gi], -jnp.inf)
        acc_refs[gi][...] = jnp.zeros_like(acc_refs[gi])

    all_groups = tuple(range(n_grp))

    def scores(j, bufs, diag, groups=all_groups):
        kt = k_ref[pl.ds(pl.multiple_of(j * t, t), t), :]
        for gi in groups:
            if diag is None or gi % nq >= diag:
                s = _dot(kt, q_grp[gi])
                s_ref, smax_ref = bufs[gi]
                s_ref[...] = s
                smax_ref[...] = jnp.max(s, axis=0, keepdims=True)

    def update(bufs, j, diag, groups=all_groups):
        vt = vT_ref[0, j]
        for gi in groups:
            u = gi % nq
            if diag is not None and u < diag:
                continue
            s_ref, smax_ref = bufs[gi]
            s = s_ref[...]
            if diag is not None and u == diag:
                kc = lax.broadcasted_iota(jnp.int32, s.shape, 0) // CHUNK
                qc = lax.broadcasted_iota(jnp.int32, s.shape, 1) // CHUNK
                s = jnp.where(kc <= qc, s, -jnp.inf)
                s_max = jnp.max(s, axis=0, keepdims=True)
            else:
                s_max = smax_ref[...]
            m_prev = m_refs[gi][...]
            m_new = jnp.maximum(m_prev, s_max)
            alpha = jnp.exp2(m_prev - m_new)
            p = jnp.exp2(s - m_new).astype(BF16)
            acc_refs[gi][...] = alpha * acc_refs[gi][...] + _dot(vt, p)
            m_refs[gi][...] = m_new

    scores(0, sa_refs, None)

    def pair(j):
        for gi in all_groups:
            scores(j + 1, sb_refs, None, (gi,))
            update(sa_refs, j, None, (gi,))
        for gi in all_groups:
            scores(j + 2, sa_refs, None, (gi,))
            update(sb_refs, j + 1, None, (gi,))

    assert nq == 4

    def double_pair(j):
        pair(j)
        pair(j + 2)

    def trip(g, carry):
        double_pair(8 * g)
        double_pair(8 * g + 4)
        return carry

    lax.fori_loop(0, i // 2, trip, 0)

    @pl.when(i % 2 == 1)
    def _():
        double_pair(4 * (i - 1))

    first_diag = nq * i
    bufs = (sa_refs, sb_refs)
    for e in range(nq):
        for gi in all_groups:
            if e + 1 < nq:
                scores(first_diag + e + 1, bufs[(e + 1) % 2], e + 1, (gi,))
            update(bufs[e % 2], first_diag + e, e, (gi,))

    lam = (jnp.exp(jnp.sum(lq1_ref[...] * lk1_ref[...], axis=-1, keepdims=True))
           - jnp.exp(jnp.sum(lq2_ref[...] * lk2_ref[...], axis=-1, keepdims=True))
           + lam_init)
    for u in range(nq):
        a1 = acc_refs[u][...]
        a2 = acc_refs[nq + u][...]
        ao = (a1[:DA_V_DIM, :] / a1[DA_V_DIM:DA_V_DIM + 1, :]
              - lam * (a2[:DA_V_DIM, :] / a2[DA_V_DIM:DA_V_DIM + 1, :]))
        y = ao * lax.rsqrt(jnp.mean(ao * ao, axis=0, keepdims=True) + EPS)
        o_ref[u * t:(u + 1) * t, :] = (y.T * g_ref[...] * (1.0 - lam_init)).astype(o_ref.dtype)


def _split3(x):
    hi = x.astype(BF16).astype(F32)
    r = x - hi
    mid = r.astype(BF16).astype(F32)
    lo = (r - mid).astype(BF16).astype(F32)
    return hi, mid, lo


def _mlstm_block(qT, k, vT, om, gates, g_row, state_ref, m_ref):
    L = k.shape[0]
    s_idx = lax.broadcasted_iota(jnp.int32, (L, L), 0)
    t_idx = lax.broadcasted_iota(jnp.int32, (L, L), 1)
    causal = s_idx <= t_idx
    tri = jnp.where(t_idx <= s_idx, 1.0, 0.0).astype(BF16)
    parts = _dot(tri, jnp.concatenate(_split3(gates), axis=1).astype(BF16))
    cum = parts[:, :LANES] + parts[:, LANES:2 * LANES] + parts[:, 2 * LANES:]
    gatesT = gates.T
    cumT = cum.T

    heads = range(ML_HEADS)
    sl = [slice(h * ML_DIM, (h + 1) * ML_DIM) for h in heads]
    state = [state_ref[h] for h in heads]
    b_row = [cumT[ML_HEADS + h:ML_HEADS + h + 1, :] for h in heads]
    i_row = [gatesT[h:h + 1, :] for h in heads]
    m_prev = [m_ref[h, 0:1, 0:1] for h in heads]

    s_raw = [_dot(k[:, sl[h]], qT[h]) for h in heads]
    inter_nd = [_dot(state[h].astype(BF16), qT[h]) for h in heads]

    m_t, w_inter, sT = [], [], []
    for h in heads:
        c_col = gates[:, h:h + 1] - cum[:, ML_HEADS + h:ML_HEADS + h + 1]
        d = jnp.where(causal, c_col + b_row[h], -jnp.inf)
        inter = b_row[h] + m_prev[h]
        m_t.append(jnp.maximum(inter, jnp.max(d, axis=0, keepdims=True)))
        w_inter.append(jnp.exp(inter - m_t[h]))
        sT.append((s_raw[h] * jnp.exp(d - m_t[h])).astype(BF16))

    outs = []
    for h in heads:
        nd = w_inter[h] * inter_nd[h] + _dot(vT[h], sT[h])
        num = nd[:ML_DIM, :]
        den = nd[ML_DIM:ML_DIM + 1, :]
        hh = num / jnp.maximum(jnp.abs(den), jnp.exp(-m_t[h]))
        mu = jnp.mean(hh, axis=0, keepdims=True)
        xc = hh - mu
        var = jnp.mean(xc * xc, axis=0, keepdims=True)
        y = (xc * lax.rsqrt(var + EPS)).T * g_row
        outs.append(y * _sigmoid(om[:, sl[h]]))

    for h in heads:
        b_last = b_row[h][:, L - 1:L]
        g_row = b_last - b_row[h] + i_row[h]
        m_new = jnp.maximum(b_last + m_prev[h], jnp.max(g_row, axis=1, keepdims=True))
        wk = jnp.exp(g_row - m_new)
        decay = jnp.exp(b_last + m_prev[h] - m_new)
        vTw = (vT[h].astype(F32) * wk).astype(BF16)
        state_ref[h] = decay * state[h] + _dot(vTw, k[:, sl[h]])
        m_ref[h] = jnp.broadcast_to(m_new, m_ref.shape[1:])
    return outs


def _gelu_tanh(x):
    return 0.5 * x * (1.0 + jnp.tanh(math.sqrt(2.0 / math.pi) * (x + 0.044715 * (x * x * x))))


def _post_kernel(x_ref, ao_ref, hm_ref, ga_ref, gb_ref, p_ref, wya_ref, wyb_ref, wo_ref, n2g_ref,
                 wup_ref, convw_ref, convb_ref, wdown_ref, wple_ref, wpg_ref, fg_ref,
                 o_ref, carry_ref, abuf_ref, u_ref):
    i = pl.program_id(0)
    tm = x_ref.shape[0]
    fc = abuf_ref.shape[1]
    d_ff = wdown_ref.shape[0]
    n_chunks = d_ff // fc

    @pl.when(i == 0)
    def _():
        carry_ref[...] = jnp.zeros_like(carry_ref)

    ya = _dot(ao_ref[...], wya_ref[...])
    yb = _dot(hm_ref[...], wyb_ref[...])
    merged = _sigmoid(ga_ref[...].astype(F32)) * ya + _sigmoid(gb_ref[...].astype(F32)) * yb
    x1 = x_ref[...] + _dot(merged.astype(BF16), wo_ref[...])

    h2 = _rmsnorm(x1, n2g_ref[...]).astype(BF16)
    def up(c):
        return (_dot(h2, wup_ref[:, c * fc:(c + 1) * fc]),
                _dot(h2, wup_ref[:, d_ff + c * fc:d_ff + (c + 1) * fc]))

    nxt = up(0)
    for c in range(n_chunks):
        cs = slice(c * fc, (c + 1) * fc)
        a, g = nxt
        if c + 1 < n_chunks:
            nxt = up(c + 1)
        abuf_ref[0:SUBLANES, :] = carry_ref[c]
        abuf_ref[SUBLANES:, :] = a
        carry_ref[c] = a[tm - SUBLANES:, :]
        cw = convw_ref[:, cs]
        ac = (cw[2:3, :] * a
              + cw[1:2, :] * abuf_ref[pl.ds(SUBLANES - 1, tm), :]
              + cw[0:1, :] * abuf_ref[pl.ds(SUBLANES - 2, tm), :]
              + convb_ref[:, cs])
        u_ref[:, cs] = (_gelu_tanh(ac) * g).astype(BF16)
    x2 = x1 + _dot(u_ref[...], wdown_ref[...])

    pe = _dot(p_ref[...].astype(BF16), wple_ref[...])
    pg = _dot(x2.astype(BF16), wpg_ref[...])
    x3 = x2 + pe * _sigmoid(pg)
    o_ref[...] = _rmsnorm(x3, fg_ref[...])


def _compiler_params(n_axes):
    return pltpu.CompilerParams(dimension_semantics=("arbitrary",) * n_axes,
                                vmem_limit_bytes=VMEM_LIMIT_BYTES)


def _rope_tables(S, tm):
    half = DA_QK_DIM // 2
    reps = LANES // half
    inv_freq = ROPE_THETA ** (-jnp.arange(0, DA_QK_DIM, 2, dtype=F32) / DA_QK_DIM)
    freq = jnp.tile(inv_freq, reps)[None, :]
    sign = jnp.tile(jnp.concatenate([-jnp.ones(half, F32), jnp.ones(half, F32)]), reps // 2)[None, :]
    in_tile = jnp.arange(tm, dtype=F32)[:, None] * freq
    base = (jnp.arange(S // tm, dtype=F32) * tm)[:, None] * freq
    cb, sb = jnp.cos(base), jnp.sin(base)
    zeros = jnp.zeros_like(cb)
    base_rows = jnp.stack([cb, sb, sign * cb, sign * sb] + [zeros] * (SUBLANES - 4), axis=1)
    return base_rows, jnp.cos(in_tile), jnp.sin(in_tile)


def kernel(x, p, norm1_g, w_in, b_if, conv_m_w, lam_q1, lam_k1, lam_q2, lam_k2, da_norm_g,
           ml_norm_g, w_ya, w_yb, w_o, norm2_g, w_up, conv_f_w, conv_f_b, w_down, w_ple, w_pg,
           final_g):
    B, S, D = x.shape
    depth = w_in.shape[0]
    assert B == 1 and depth == 1
    d_ff = w_down.shape[1]
    qk_w = DA_HEADS * 2 * DA_QK_DIM
    v_w = DA_HEADS * DA_V_DIM
    ml_w = ML_HEADS * ML_DIM
    lam_init = 0.8 - 0.6 * math.exp(-0.3 * 0)

    tm1 = min(PROJ_ROWS, S)
    tile = min(ATTN_TILE, S)
    ml_blk = min(ML_BLOCK, S)
    tm4 = min(POST_ROWS, S)
    assert S % tm1 == 0 and S % tile == 0 and S % tm4 == 0 and tm1 % tile == 0
    assert ml_blk == tile and DA_V_DIM == ML_DIM
    assert d_ff % FF_CHUNK == 0
    n_tiles = S // tile
    tq = ATTN_Q_TILES * tile
    n_grp = 2 * ATTN_Q_TILES
    assert S % tq == 0
    v_rows = DA_V_DIM + V_PAD_ROWS

    x2d = x[0]
    wi = w_in[0]
    if_lo = 2 * qk_w + v_w + 4 * ml_w
    if_hi = if_lo + 2 * ML_HEADS
    w_main = wi[:, :if_lo].astype(BF16)
    w_if = jnp.pad(wi[:, if_lo:if_hi], ((0, 0), (0, LANES - 2 * ML_HEADS))).astype(BF16)
    w_g = wi[:, if_hi:].astype(BF16)
    bif = jnp.pad(b_if[0], (0, LANES - 2 * ML_HEADS))[None, :]
    rope_base, rope_cos, rope_sin = _rope_tables(S, tm1)

    row1 = lambda a: a.reshape(1, -1)

    proj_out_shapes = (
        jax.ShapeDtypeStruct((DA_HEADS, n_tiles, LANES, tile), BF16),
        jax.ShapeDtypeStruct((S, qk_w), BF16),
        jax.ShapeDtypeStruct((DA_HEADS, n_tiles, v_rows, tile), BF16),
        jax.ShapeDtypeStruct((S, ml_w), BF16),
        jax.ShapeDtypeStruct((S, D), BF16),
        jax.ShapeDtypeStruct((S, D), BF16),
    )
    rows = lambda w: pl.BlockSpec((tm1, w), lambda i: (i, 0))
    tspec = pl.BlockSpec((DA_HEADS, tm1 // tile, LANES, tile), lambda i: (0, i, 0, 0))
    vspec = pl.BlockSpec((DA_HEADS, tm1 // tile, v_rows, tile), lambda i: (0, i, 0, 0))
    qT, k_a, vT, hm, ga, gb = pl.pallas_call(
        functools.partial(_proj_kernel, tile=tile),
        grid=(S // tm1,),
        in_specs=[rows(D), _const_spec((1, D)),
                  pl.BlockSpec((1, SUBLANES, LANES), lambda i: (i, 0, 0)),
                  _const_spec((tm1, LANES)), _const_spec((tm1, LANES)),
                  _const_spec(w_main.shape), _const_spec(w_if.shape), _const_spec(w_g.shape),
                  _const_spec((1, LANES)), _const_spec(conv_m_w.shape[1:]),
                  _const_spec((1, ML_DIM))],
        out_specs=(tspec, rows(qk_w), vspec, rows(ml_w), rows(D), rows(D)),
        out_shape=proj_out_shapes,
        scratch_shapes=[pltpu.VMEM((SUBLANES, 2 * ml_w), F32),
                        pltpu.VMEM((tm1 + SUBLANES, 2 * ml_w), F32),
                        pltpu.VMEM((ML_HEADS, v_rows, ML_DIM), F32),
                        pltpu.VMEM((ML_HEADS, SUBLANES, LANES), F32)],
        compiler_params=_compiler_params(1),
        name="proj",
    )(x2d, row1(norm1_g[0]), rope_base, rope_cos, rope_sin, w_main, w_if, w_g, bif, conv_m_w[0],
      row1(ml_norm_g[0]))

    ao = pl.pallas_call(
        functools.partial(_attn_kernel, lam_init=lam_init),
        grid=(DA_HEADS, S // tq),
        in_specs=[pl.BlockSpec((1, ATTN_Q_TILES, LANES, tile), lambda h, i: (h, i, 0, 0)),
                  pl.BlockSpec((S, LANES), lambda h, i: (0, h)),
                  pl.BlockSpec((1, n_tiles, v_rows, tile), lambda h, i: (h, 0, 0, 0)),
                  _const_spec((1, DA_QK_DIM)), _const_spec((1, DA_QK_DIM)),
                  _const_spec((1, DA_QK_DIM)), _const_spec((1, DA_QK_DIM)),
                  _const_spec((1, DA_V_DIM))],
        out_specs=pl.BlockSpec((tq, LANES), lambda h, i: (i, h)),
        out_shape=jax.ShapeDtypeStruct((S, v_w), BF16),
        scratch_shapes=([pltpu.VMEM((v_rows, tile), F32)] * n_grp
                        + [pltpu.VMEM((1, tile), F32)] * n_grp
                        + [pltpu.VMEM((tile, tile), F32)] * (2 * n_grp)
                        + [pltpu.VMEM((1, tile), F32)] * (2 * n_grp)),
        compiler_params=_compiler_params(2),
        name="diff_attn",
    )(qT, k_a, vT, row1(lam_q1[0]), row1(lam_k1[0]), row1(lam_q2[0]), row1(lam_k2[0]),
      row1(da_norm_g[0]))

    n_chunks = d_ff // FF_CHUNK
    conv_w = conv_f_w[0]
    conv_b = row1(conv_f_b[0])
    w_dn = w_down[0].astype(BF16)
    prow = lambda w: pl.BlockSpec((tm4, w), lambda i: (i, 0))
    consts = (w_ya[0].astype(BF16), w_yb[0].astype(BF16), w_o[0].astype(BF16), row1(norm2_g[0]),
              w_up[0].astype(BF16), conv_w, conv_b, w_dn, w_ple[0].astype(BF16), w_pg[0].astype(BF16),
              row1(final_g))
    out = pl.pallas_call(
        _post_kernel,
        grid=(S // tm4,),
        in_specs=[prow(D), prow(v_w), prow(ml_w), prow(D), prow(D), prow(p.shape[-1])]
                 + [_const_spec(a.shape) for a in consts],
        out_specs=prow(D),
        out_shape=jax.ShapeDtypeStruct((S, D), x.dtype),
        scratch_shapes=[pltpu.VMEM((n_chunks, SUBLANES, FF_CHUNK), F32),
                        pltpu.VMEM((tm4 + SUBLANES, FF_CHUNK), F32),
                        pltpu.VMEM((tm4, d_ff), BF16)],
        compiler_params=_compiler_params(1),
        name="post",
    )(x2d, ao, hm, ga, gb, p[0, 0], *consts)
    return out[None]
```

```python
import functools
import math

import jax
import jax.numpy as jnp
from jax import lax
from jax.experimental import pallas as pl
from jax.experimental.pallas import tpu as pltpu

F32 = jnp.float32
BF16 = jnp.bfloat16

CHUNK = 64
ROPE_THETA = 10000.0
DA_HEADS = 4
DA_QK_DIM = 64
DA_V_DIM = 128
ML_HEADS = 4
ML_DIM = 128
EPS = 1e-6
LOG2E = math.log2(math.e)
V_PAD_ROWS = 16
LANES = 128
SUBLANES = 8
VMEM_LIMIT_BYTES = 56 * 1024 * 1024

PROJ_ROWS = 512
ATTN_TILE = 256
ATTN_Q_TILES = 4
ML_BLOCK = 256
POST_ROWS = 512
FF_CHUNK = 256


def _dot(a, b):
    return jnp.dot(a, b, preferred_element_type=F32)


def _sigmoid(x):
    return 1.0 / (1.0 + jnp.exp(-x))


def _rmsnorm(x, g):
    return x * lax.rsqrt(jnp.mean(x * x, axis=-1, keepdims=True) + EPS) * g


def _const_spec(shape):
    nd = len(shape)
    return pl.BlockSpec(shape, lambda *_: (0,) * nd)


def _proj_kernel(x_ref, g1_ref, rope_base_ref, rope_cos_ref, rope_sin_ref, w_ref, wtail_ref,
                 bif_ref, convw_ref, mlg_ref,
                 qT_ref, k_ref, vT_ref, hm_ref, ga_ref, gb_ref,
                 carry_ref, cbuf_ref, state_ref, m_ref, wif_ref, wg_ref, *, tile):
    i = pl.program_id(0)

    @pl.when(i == 0)
    def _():
        n_gate = 2 * ML_HEADS
        gate_cols = wtail_ref[:, 0:LANES]
        gate_lane = lax.broadcasted_iota(jnp.int32, gate_cols.shape, 1)
        wif_ref[...] = jnp.where(gate_lane < n_gate, gate_cols, 0.0).astype(BF16)
        wg_ref[...] = wtail_ref[:, n_gate:].astype(BF16)
    tm = x_ref.shape[0]
    n_sub = tm // tile
    hb = _rmsnorm(x_ref[...], g1_ref[...]).astype(BF16)

    qk_w = 2 * DA_HEADS * 2 * DA_QK_DIM
    v_w = DA_HEADS * DA_V_DIM
    ml_w = ML_HEADS * ML_DIM
    d_model = x_ref.shape[1]
    bounds = [0]
    for width in (qk_w, v_w, 2 * ml_w, ml_w, ml_w):
        bounds.append(bounds[-1] + width)

    def proj(n):
        return _dot(hb, w_ref[:, bounds[n]:bounds[n + 1]])

    pad_row = lax.broadcasted_iota(jnp.int32, (V_PAD_ROWS, tm), 0)
    ones_rows = jnp.where(pad_row == 0, 1.0, 0.0).astype(BF16)

    def transposed(cols, extra_rows=None):
        cT = cols.astype(BF16).T
        return cT if extra_rows is None else jnp.concatenate([cT, extra_rows], axis=0)

    def put_transposed(dst_ref, h, cols, extra_rows=None):
        cT = transposed(cols, extra_rows)
        for s in range(n_sub):
            dst_ref[h, s] = cT[:, s * tile:(s + 1) * tile]

    lane = lax.broadcasted_iota(jnp.int32, (tm, LANES), 1)
    gp = _dot(hb, wif_ref[...]) + bif_ref[...]
    logsig = jnp.minimum(gp, 0.0) - jnp.log1p(jnp.exp(-jnp.abs(gp)))
    gates = jnp.where(lane < ML_HEADS, gp, logsig)

    qk = proj(0)
    base = rope_base_ref[0]
    cos_r = rope_cos_ref[...]
    sin_r = rope_sin_ref[...]
    cos = base[0:1, :] * cos_r - base[1:2, :] * sin_r
    sin = base[3:4, :] * cos_r + base[2:3, :] * sin_r
    first_half = (lane % DA_QK_DIM) < (DA_QK_DIM // 2)
    n_grp = DA_HEADS
    for c in range(2 * n_grp):
        t = qk[:, c * LANES:(c + 1) * LANES]
        partner = jnp.where(first_half, pltpu.roll(t, LANES - DA_QK_DIM // 2, 1),
                            pltpu.roll(t, DA_QK_DIM // 2, 1))
        r = t * cos + partner * sin
        if c < n_grp:
            put_transposed(qT_ref, c, r * (DA_QK_DIM ** -0.5 * LOG2E))
        else:
            k_ref[:, (c - n_grp) * LANES:(c - n_grp + 1) * LANES] = r.astype(BF16)

    v = proj(1)
    for h in range(DA_HEADS):
        put_transposed(vT_ref, h, v[:, h * LANES:(h + 1) * LANES], ones_rows)

    @pl.when(i == 0)
    def _():
        carry_ref[...] = jnp.zeros_like(carry_ref)
        state_ref[...] = jnp.zeros_like(state_ref)
        m_ref[...] = jnp.zeros_like(m_ref)

    raw = proj(2)
    vm = proj(3)
    cbuf_ref[0:SUBLANES, :] = carry_ref[...]
    cbuf_ref[SUBLANES:, :] = raw
    carry_ref[...] = raw[tm - SUBLANES:, :]
    cw = convw_ref[...]
    y = (cw[3:4, :] * raw
         + cw[2:3, :] * cbuf_ref[pl.ds(SUBLANES - 1, tm), :]
         + cw[1:2, :] * cbuf_ref[pl.ds(SUBLANES - 2, tm), :]
         + cw[0:1, :] * cbuf_ref[pl.ds(SUBLANES - 3, tm), :])
    y = y * _sigmoid(y)
    om = proj(4)

    km = (y[:, ml_w:] * (ML_DIM ** -0.5)).astype(BF16)
    qmT = [transposed(y[:, h * ML_DIM:(h + 1) * ML_DIM]) for h in range(ML_HEADS)]
    vmT = [transposed(vm[:, h * ML_DIM:(h + 1) * ML_DIM], ones_rows) for h in range(ML_HEADS)]
    for s in range(n_sub):
        if s == 0:
            ga_ref[...] = _dot(hb, wg_ref[:, :d_model]).astype(BF16)
        if s == n_sub - 1:
            gb_ref[...] = _dot(hb, wg_ref[:, d_model:]).astype(BF16)
        rs = slice(s * tile, (s + 1) * tile)
        outs = _mlstm_block([q[:, rs] for q in qmT], km[rs, :], [v[:, rs] for v in vmT],
                            om[rs, :], gates[rs, :], mlg_ref[...], state_ref, m_ref)
        for h in range(ML_HEADS):
            hm_ref[rs, h * ML_DIM:(h + 1) * ML_DIM] = outs[h].astype(hm_ref.dtype)


def _attn_kernel(qT_ref, k_ref, vT_ref, lq1_ref, lk1_ref, lq2_ref, lk2_ref, g_ref, o_ref,
                 *scratch, lam_init):
    nq = ATTN_Q_TILES
    n_grp = 2 * nq
    acc_refs = scratch[0:n_grp]
    m_refs = scratch[n_grp:2 * n_grp]
    sa_refs = list(zip(scratch[2 * n_grp:3 * n_grp], scratch[4 * n_grp:5 * n_grp]))
    sb_refs = list(zip(scratch[3 * n_grp:4 * n_grp], scratch[5 * n_grp:6 * n_grp]))
    i = pl.program_id(1)
    t = qT_ref.shape[3]
    row = lax.broadcasted_iota(jnp.int32, (qT_ref.shape[2], t), 0)
    q_grp = []
    for mp in range(2):
        keep = (row < DA_QK_DIM) if mp == 0 else (row >= DA_QK_DIM)
        for u in range(nq):
            q = qT_ref[0, u]
            q_grp.append(jnp.where(keep, q, jnp.zeros_like(q)))

    for gi in range(n_grp):
        m_refs[gi][...] = jnp.full_like(m_refs[gi], -jnp.inf)
        acc_refs[gi][...] = jnp.zeros_like(acc_refs[gi])

    all_groups = tuple(range(n_grp))

    def scores(j, bufs, diag, groups=all_groups):
        kt = k_ref[pl.ds(pl.multiple_of(j * t, t), t), :]
        for gi in groups:
            if diag is None or gi % nq >= diag:
                s = _dot(kt, q_grp[gi])
                s_ref, smax_ref = bufs[gi]
                s_ref[...] = s
                smax_ref[...] = jnp.max(s, axis=0, keepdims=True)

    def update(bufs, j, diag, groups=all_groups):
        vt = vT_ref[0, j]
        for gi in groups:
            u = gi % nq
            if diag is not None and u < diag:
                continue
            s_ref, smax_ref = bufs[gi]
            s = s_ref[...]
            if diag is not None and u == diag:
                kc = lax.broadcasted_iota(jnp.int32, s.shape, 0) // CHUNK
                qc = lax.broadcasted_iota(jnp.int32, s.shape, 1) // CHUNK
                s = jnp.where(kc <= qc, s, -jnp.inf)
                s_max = jnp.max(s, axis=0, keepdims=True)
            else:
                s_max = smax_ref[...]
            m_prev = m_refs[gi][...]
            m_new = jnp.maximum(m_prev, s_max)
            alpha = jnp.exp2(m_prev - m_new)
            p = jnp.exp2(s - m_new).astype(BF16)
            acc_refs[gi][...] = alpha * acc_refs[gi][...] + _dot(vt, p)
            m_refs[gi][...] = m_new

    scores(0, sa_refs, None)

    def pair(j):
        for gi in all_groups:
            scores(j + 1, sb_refs, None, (gi,))
            update(sa_refs, j, None, (gi,))
        for gi in all_groups:
            scores(j + 2, sa_refs, None, (gi,))
            update(sb_refs, j + 1, None, (gi,))

    assert nq == 4

    def double_pair(j):
        pair(j)
        pair(j + 2)

    def trip(g, carry):
        double_pair(8 * g)
        double_pair(8 * g + 4)
        return carry

    lax.fori_loop(0, i // 2, trip, 0)

    @pl.when(i % 2 == 1)
    def _():
        double_pair(4 * (i - 1))

    first_diag = nq * i
    bufs = (sa_refs, sb_refs)
    for e in range(nq):
        for gi in all_groups:
            if e + 1 < nq:
                scores(first_diag + e + 1, bufs[(e + 1) % 2], e + 1, (gi,))
            update(bufs[e % 2], first_diag + e, e, (gi,))

    lam = (jnp.exp(jnp.sum(lq1_ref[...] * lk1_ref[...], axis=-1, keepdims=True))
           - jnp.exp(jnp.sum(lq2_ref[...] * lk2_ref[...], axis=-1, keepdims=True))
           + lam_init)
    for u in range(nq):
        a1 = acc_refs[u][...]
        a2 = acc_refs[nq + u][...]
        ao = (a1[:DA_V_DIM, :] / a1[DA_V_DIM:DA_V_DIM + 1, :]
              - lam * (a2[:DA_V_DIM, :] / a2[DA_V_DIM:DA_V_DIM + 1, :]))
        y = ao * lax.rsqrt(jnp.mean(ao * ao, axis=0, keepdims=True) + EPS)
        o_ref[u * t:(u + 1) * t, :] = (y.T * g_ref[...] * (1.0 - lam_init)).astype(o_ref.dtype)


def _split3(x):
    hi = x.astype(BF16).astype(F32)
    r = x - hi
    mid = r.astype(BF16).astype(F32)
    lo = (r - mid).astype(BF16).astype(F32)
    return hi, mid, lo


def _mlstm_block(qT, k, vT, om, gates, g_row, state_ref, m_ref):
    L = k.shape[0]
    s_idx = lax.broadcasted_iota(jnp.int32, (L, L), 0)
    t_idx = lax.broadcasted_iota(jnp.int32, (L, L), 1)
    causal = s_idx <= t_idx
    tri = jnp.where(t_idx <= s_idx, 1.0, 0.0).astype(BF16)
    parts = _dot(tri, jnp.concatenate(_split3(gates), axis=1).astype(BF16))
    cum = parts[:, :LANES] + parts[:, LANES:2 * LANES] + parts[:, 2 * LANES:]
    gatesT = gates.T
    cumT = cum.T

    heads = range(ML_HEADS)
    sl = [slice(h * ML_DIM, (h + 1) * ML_DIM) for h in heads]
    state = [state_ref[h] for h in heads]
    b_row = [cumT[ML_HEADS + h:ML_HEADS + h + 1, :] for h in heads]
    i_row = [gatesT[h:h + 1, :] for h in heads]
    m_prev = [m_ref[h, 0:1, 0:1] for h in heads]

    s_raw = [_dot(k[:, sl[h]], qT[h]) for h in heads]
    inter_nd = [_dot(state[h].astype(BF16), qT[h]) for h in heads]

    m_t, w_inter, sT = [], [], []
    for h in heads:
        c_col = gates[:, h:h + 1] - cum[:, ML_HEADS + h:ML_HEADS + h + 1]
        d = jnp.where(causal, c_col + b_row[h], -jnp.inf)
        inter = b_row[h] + m_prev[h]
        m_t.append(jnp.maximum(inter, jnp.max(d, axis=0, keepdims=True)))
        w_inter.append(jnp.exp(inter - m_t[h]))
        sT.append((s_raw[h] * jnp.exp(d - m_t[h])).astype(BF16))

    outs = []
    for h in heads:
        nd = w_inter[h] * inter_nd[h] + _dot(vT[h], sT[h])
        num = nd[:ML_DIM, :]
        den = nd[ML_DIM:ML_DIM + 1, :]
        hh = num / jnp.maximum(jnp.abs(den), jnp.exp(-m_t[h]))
        mu = jnp.mean(hh, axis=0, keepdims=True)
        xc = hh - mu
        var = jnp.mean(xc * xc, axis=0, keepdims=True)
        y = (xc * lax.rsqrt(var + EPS)).T * g_row
        outs.append(y * _sigmoid(om[:, sl[h]]))

    for h in heads:
        b_last = b_row[h][:, L - 1:L]
        g_row = b_last - b_row[h] + i_row[h]
        m_new = jnp.maximum(b_last + m_prev[h], jnp.max(g_row, axis=1, keepdims=True))
        wk = jnp.exp(g_row - m_new)
        decay = jnp.exp(b_last + m_prev[h] - m_new)
        vTw = (vT[h].astype(F32) * wk).astype(BF16)
        state_ref[h] = decay * state[h] + _dot(vTw, k[:, sl[h]])
        m_ref[h] = jnp.broadcast_to(m_new, m_ref.shape[1:])
    return outs


def _gelu_tanh(x):
    return 0.5 * x * (1.0 + jnp.tanh(math.sqrt(2.0 / math.pi) * (x + 0.044715 * (x * x * x))))


def _post_kernel(x_ref, ao_ref, hm_ref, ga_ref, gb_ref, p_ref, wya_ref, wyb_ref, wo_ref, n2g_ref,
                 wup_ref, convw_ref, convb_ref, wdown_ref, wple_ref, wpg_ref, fg_ref,
                 o_ref, carry_ref, abuf_ref, u_ref):
    i = pl.program_id(0)
    tm = x_ref.shape[0]
    fc = abuf_ref.shape[1]
    d_ff = wdown_ref.shape[0]
    n_chunks = d_ff // fc

    @pl.when(i == 0)
    def _():
        carry_ref[...] = jnp.zeros_like(carry_ref)

    ya = _dot(ao_ref[...], wya_ref[...])
    yb = _dot(hm_ref[...], wyb_ref[...])
    merged = _sigmoid(ga_ref[...].astype(F32)) * ya + _sigmoid(gb_ref[...].astype(F32)) * yb
    x1 = x_ref[...] + _dot(merged.astype(BF16), wo_ref[...])

    h2 = _rmsnorm(x1, n2g_ref[...]).astype(BF16)
    def up(c):
        return (_dot(h2, wup_ref[:, c * fc:(c + 1) * fc]),
                _dot(h2, wup_ref[:, d_ff + c * fc:d_ff + (c + 1) * fc]))

    nxt = up(0)
    for c in range(n_chunks):
        cs = slice(c * fc, (c + 1) * fc)
        a, g = nxt
        if c + 1 < n_chunks:
            nxt = up(c + 1)
        abuf_ref[0:SUBLANES, :] = carry_ref[c]
        abuf_ref[SUBLANES:, :] = a
        carry_ref[c] = a[tm - SUBLANES:, :]
        cw = convw_ref[:, cs]
        ac = (cw[2:3, :] * a
              + cw[1:2, :] * abuf_ref[pl.ds(SUBLANES - 1, tm), :]
              + cw[0:1, :] * abuf_ref[pl.ds(SUBLANES - 2, tm), :]
              + convb_ref[:, cs])
        u_ref[:, cs] = (_gelu_tanh(ac) * g).astype(BF16)
    x2 = x1 + _dot(u_ref[...], wdown_ref[...])

    pe = _dot(p_ref[...].astype(BF16), wple_ref[...])
    pg = _dot(x2.astype(BF16), wpg_ref[...])
    x3 = x2 + pe * _sigmoid(pg)
    o_ref[...] = _rmsnorm(x3, fg_ref[...])


def _compiler_params(n_axes):
    return pltpu.CompilerParams(dimension_semantics=("arbitrary",) * n_axes,
                                vmem_limit_bytes=VMEM_LIMIT_BYTES)


def _rope_tables(S, tm):
    half = DA_QK_DIM // 2
    reps = LANES // half
    inv_freq = ROPE_THETA ** (-jnp.arange(0, DA_QK_DIM, 2, dtype=F32) / DA_QK_DIM)
    freq = jnp.tile(inv_freq, reps)[None, :]
    sign = jnp.tile(jnp.concatenate([-jnp.ones(half, F32), jnp.ones(half, F32)]), reps // 2)[None, :]
    in_tile = jnp.arange(tm, dtype=F32)[:, None] * freq
    base = (jnp.arange(S // tm, dtype=F32) * tm)[:, None] * freq
    cb, sb = jnp.cos(base), jnp.sin(base)
    zeros = jnp.zeros_like(cb)
    base_rows = jnp.stack([cb, sb, sign * cb, sign * sb] + [zeros] * (SUBLANES - 4), axis=1)
    return base_rows, jnp.cos(in_tile), jnp.sin(in_tile)


def kernel(x, p, norm1_g, w_in, b_if, conv_m_w, lam_q1, lam_k1, lam_q2, lam_k2, da_norm_g,
           ml_norm_g, w_ya, w_yb, w_o, norm2_g, w_up, conv_f_w, conv_f_b, w_down, w_ple, w_pg,
           final_g):
    B, S, D = x.shape
    depth = w_in.shape[0]
    assert B == 1 and depth == 1
    d_ff = w_down.shape[1]
    qk_w = DA_HEADS * 2 * DA_QK_DIM
    v_w = DA_HEADS * DA_V_DIM
    ml_w = ML_HEADS * ML_DIM
    lam_init = 0.8 - 0.6 * math.exp(-0.3 * 0)

    tm1 = min(PROJ_ROWS, S)
    tile = min(ATTN_TILE, S)
    ml_blk = min(ML_BLOCK, S)
    tm4 = min(POST_ROWS, S)
    assert S % tm1 == 0 and S % tile == 0 and S % tm4 == 0 and tm1 % tile == 0
    assert ml_blk == tile and DA_V_DIM == ML_DIM
    assert d_ff % FF_CHUNK == 0
    n_tiles = S // tile
    tq = ATTN_Q_TILES * tile
    n_grp = 2 * ATTN_Q_TILES
    assert S % tq == 0
    v_rows = DA_V_DIM + V_PAD_ROWS

    x2d = x[0]
    wi = w_in[0]
    if_lo = 2 * qk_w + v_w + 4 * ml_w
    if_hi = if_lo + 2 * ML_HEADS
    w_main = wi[:, :if_lo].astype(BF16)
    w_tail = wi[:, if_lo:]
    bif = jnp.pad(b_if[0], (0, LANES - 2 * ML_HEADS))[None, :]
    rope_base, rope_cos, rope_sin = _rope_tables(S, tm1)

    row1 = lambda a: a.reshape(1, -1)

    proj_out_shapes = (
        jax.ShapeDtypeStruct((DA_HEADS, n_tiles, LANES, tile), BF16),
        jax.ShapeDtypeStruct((S, qk_w), BF16),
        jax.ShapeDtypeStruct((DA_HEADS, n_tiles, v_rows, tile), BF16),
        jax.ShapeDtypeStruct((S, ml_w), BF16),
        jax.ShapeDtypeStruct((S, D), BF16),
        jax.ShapeDtypeStruct((S, D), BF16),
    )
    rows = lambda w: pl.BlockSpec((tm1, w), lambda i: (i, 0))
    tspec = pl.BlockSpec((DA_HEADS, tm1 // tile, LANES, tile), lambda i: (0, i, 0, 0))
    vspec = pl.BlockSpec((DA_HEADS, tm1 // tile, v_rows, tile), lambda i: (0, i, 0, 0))
    qT, k_a, vT, hm, ga, gb = pl.pallas_call(
        functools.partial(_proj_kernel, tile=tile),
        grid=(S // tm1,),
        in_specs=[rows(D), _const_spec((1, D)),
                  pl.BlockSpec((1, SUBLANES, LANES), lambda i: (i, 0, 0)),
                  _const_spec((tm1, LANES)), _const_spec((tm1, LANES)),
                  _const_spec(w_main.shape), _const_spec(w_tail.shape),
                  _const_spec((1, LANES)), _const_spec(conv_m_w.shape[1:]),
                  _const_spec((1, ML_DIM))],
        out_specs=(tspec, rows(qk_w), vspec, rows(ml_w), rows(D), rows(D)),
        out_shape=proj_out_shapes,
        scratch_shapes=[pltpu.VMEM((SUBLANES, 2 * ml_w), F32),
                        pltpu.VMEM((tm1 + SUBLANES, 2 * ml_w), F32),
                        pltpu.VMEM((ML_HEADS, v_rows, ML_DIM), F32),
                        pltpu.VMEM((ML_HEADS, SUBLANES, LANES), F32),
                        pltpu.VMEM((D, LANES), BF16),
                        pltpu.VMEM((D, 2 * D), BF16)],
        compiler_params=_compiler_params(1),
        name="proj",
    )(x2d, row1(norm1_g[0]), rope_base, rope_cos, rope_sin, w_main, w_tail, bif, conv_m_w[0],
      row1(ml_norm_g[0]))

    ao = pl.pallas_call(
        functools.partial(_attn_kernel, lam_init=lam_init),
        grid=(DA_HEADS, S // tq),
        in_specs=[pl.BlockSpec((1, ATTN_Q_TILES, LANES, tile), lambda h, i: (h, i, 0, 0)),
                  pl.BlockSpec((S, LANES), lambda h, i: (0, h)),
                  pl.BlockSpec((1, n_tiles, v_rows, tile), lambda h, i: (h, 0, 0, 0)),
                  _const_spec((1, DA_QK_DIM)), _const_spec((1, DA_QK_DIM)),
                  _const_spec((1, DA_QK_DIM)), _const_spec((1, DA_QK_DIM)),
                  _const_spec((1, DA_V_DIM))],
        out_specs=pl.BlockSpec((tq, LANES), lambda h, i: (i, h)),
        out_shape=jax.ShapeDtypeStruct((S, v_w), BF16),
        scratch_shapes=([pltpu.VMEM((v_rows, tile), F32)] * n_grp
                        + [pltpu.VMEM((1, tile), F32)] * n_grp
                        + [pltpu.VMEM((tile, tile), F32)] * (2 * n_grp)
                        + [pltpu.VMEM((1, tile), F32)] * (2 * n_grp)),
        compiler_params=_compiler_params(2),
        name="diff_attn",
    )(qT, k_a, vT, row1(lam_q1[0]), row1(lam_k1[0]), row1(lam_q2[0]), row1(lam_k2[0]),
      row1(da_norm_g[0]))

    n_chunks = d_ff // FF_CHUNK
    conv_w = conv_f_w[0]
    conv_b = row1(conv_f_b[0])
    w_dn = w_down[0].astype(BF16)
    prow = lambda w: pl.BlockSpec((tm4, w), lambda i: (i, 0))
    consts = (w_ya[0].astype(BF16), w_yb[0].astype(BF16), w_o[0].astype(BF16), row1(norm2_g[0]),
              w_up[0].astype(BF16), conv_w, conv_b, w_dn, w_ple[0].astype(BF16), w_pg[0].astype(BF16),
              row1(final_g))
    out = pl.pallas_call(
        _post_kernel,
        grid=(S // tm4,),
        in_specs=[prow(D), prow(v_w), prow(ml_w), prow(D), prow(D), prow(p.shape[-1])]
                 + [_const_spec(a.shape) for a in consts],
        out_specs=prow(D),
        out_shape=jax.ShapeDtypeStruct((S, D), x.dtype),
        scratch_shapes=[pltpu.VMEM((n_chunks, SUBLANES, FF_CHUNK), F32),
                        pltpu.VMEM((tm4 + SUBLANES, FF_CHUNK), F32),
                        pltpu.VMEM((tm4, d_ff), BF16)],
        compiler_params=_compiler_params(1),
        name="post",
    )(x2d, ao, hm, ga, gb, p[0, 0], *consts)
    return out[None]
```

```python
import functools
import math

import jax
import jax.numpy as jnp
from jax import lax
from jax.experimental import pallas as pl
from jax.experimental.pallas import tpu as pltpu

F32 = jnp.float32
BF16 = jnp.bfloat16

CHUNK = 64
ROPE_THETA = 10000.0
DA_HEADS = 4
DA_QK_DIM = 64
DA_V_DIM = 128
ML_HEADS = 4
ML_DIM = 128
EPS = 1e-6
LOG2E = math.log2(math.e)
V_PAD_ROWS = 16
LANES = 128
SUBLANES = 8
VMEM_LIMIT_BYTES = 56 * 1024 * 1024

PROJ_ROWS = 1024
ATTN_TILE = 256
ATTN_Q_TILES = 4
ML_BLOCK = 256
POST_ROWS = 512
FF_CHUNK = 256


def _dot(a, b):
    return jnp.dot(a, b, preferred_element_type=F32)


def _sigmoid(x):
    return 1.0 / (1.0 + jnp.exp(-x))


def _rmsnorm(x, g):
    return x * lax.rsqrt(jnp.mean(x * x, axis=-1, keepdims=True) + EPS) * g


def _const_spec(shape):
    nd = len(shape)
    return pl.BlockSpec(shape, lambda *_: (0,) * nd)


def _proj_kernel(x_ref, g1_ref, rope_base_ref, rope_cos_ref, rope_sin_ref, w_ref, wif_ref, wg_ref,
                 bif_ref, convw_ref, mlg_ref,
                 qT_ref, k_ref, vT_ref, hm_ref, ga_ref, gb_ref,
                 carry_ref, cbuf_ref, state_ref, m_ref, *, tile):
    i = pl.program_id(0)
    tm = x_ref.shape[0]
    n_sub = tm // tile
    hb = _rmsnorm(x_ref[...], g1_ref[...]).astype(BF16)

    qk_w = 2 * DA_HEADS * 2 * DA_QK_DIM
    v_w = DA_HEADS * DA_V_DIM
    ml_w = ML_HEADS * ML_DIM
    d_model = x_ref.shape[1]
    bounds = [0]
    for width in (qk_w, v_w, 2 * ml_w, ml_w, ml_w):
        bounds.append(bounds[-1] + width)

    def proj(n):
        return _dot(hb, w_ref[:, bounds[n]:bounds[n + 1]])

    pad_row = lax.broadcasted_iota(jnp.int32, (V_PAD_ROWS, tm), 0)
    ones_rows = jnp.where(pad_row == 0, 1.0, 0.0).astype(BF16)

    def transposed(cols, extra_rows=None):
        cT = cols.astype(BF16).T
        return cT if extra_rows is None else jnp.concatenate([cT, extra_rows], axis=0)

    def put_transposed(dst_ref, h, cols, extra_rows=None):
        cT = transposed(cols, extra_rows)
        for s in range(n_sub):
            dst_ref[h, s] = cT[:, s * tile:(s + 1) * tile]

    lane = lax.broadcasted_iota(jnp.int32, (tm, LANES), 1)
    gp = _dot(hb, wif_ref[...]) + bif_ref[...]
    logsig = jnp.minimum(gp, 0.0) - jnp.log1p(jnp.exp(-jnp.abs(gp)))
    gates = jnp.where(lane < ML_HEADS, gp, logsig)

    qk = proj(0)
    base = rope_base_ref[0]
    cos_r = rope_cos_ref[...]
    sin_r = rope_sin_ref[...]
    cos = base[0:1, :] * cos_r - base[1:2, :] * sin_r
    sin = base[3:4, :] * cos_r + base[2:3, :] * sin_r
    first_half = (lane % DA_QK_DIM) < (DA_QK_DIM // 2)
    n_grp = DA_HEADS
    for c in range(2 * n_grp):
        t = qk[:, c * LANES:(c + 1) * LANES]
        partner = jnp.where(first_half, pltpu.roll(t, LANES - DA_QK_DIM // 2, 1),
                            pltpu.roll(t, DA_QK_DIM // 2, 1))
        r = t * cos + partner * sin
        if c < n_grp:
            put_transposed(qT_ref, c, r * (DA_QK_DIM ** -0.5 * LOG2E))
        else:
            k_ref[:, (c - n_grp) * LANES:(c - n_grp + 1) * LANES] = r.astype(BF16)

    v = proj(1)
    for h in range(DA_HEADS):
        put_transposed(vT_ref, h, v[:, h * LANES:(h + 1) * LANES], ones_rows)

    @pl.when(i == 0)
    def _():
        carry_ref[...] = jnp.zeros_like(carry_ref)
        state_ref[...] = jnp.zeros_like(state_ref)
        m_ref[...] = jnp.zeros_like(m_ref)

    raw = proj(2)
    vm = proj(3)
    cbuf_ref[0:SUBLANES, :] = carry_ref[...]
    cbuf_ref[SUBLANES:, :] = raw
    carry_ref[...] = raw[tm - SUBLANES:, :]
    cw = convw_ref[...]
    y = (cw[3:4, :] * raw
         + cw[2:3, :] * cbuf_ref[pl.ds(SUBLANES - 1, tm), :]
         + cw[1:2, :] * cbuf_ref[pl.ds(SUBLANES - 2, tm), :]
         + cw[0:1, :] * cbuf_ref[pl.ds(SUBLANES - 3, tm), :])
    y = y * _sigmoid(y)
    om = proj(4)

    km = (y[:, ml_w:] * (ML_DIM ** -0.5)).astype(BF16)
    qmT = [transposed(y[:, h * ML_DIM:(h + 1) * ML_DIM]) for h in range(ML_HEADS)]
    vmT = [transposed(vm[:, h * ML_DIM:(h + 1) * ML_DIM], ones_rows) for h in range(ML_HEADS)]
    for s in range(n_sub):
        if s == 0:
            ga_ref[...] = _dot(hb, wg_ref[:, :d_model]).astype(BF16)
        if s == n_sub - 1:
            gb_ref[...] = _dot(hb, wg_ref[:, d_model:]).astype(BF16)
        rs = slice(s * tile, (s + 1) * tile)
        outs = _mlstm_block([q[:, rs] for q in qmT], km[rs, :], [v[:, rs] for v in vmT],
                            om[rs, :], gates[rs, :], mlg_ref[...], state_ref, m_ref)
        for h in range(ML_HEADS):
            hm_ref[rs, h * ML_DIM:(h + 1) * ML_DIM] = outs[h].astype(hm_ref.dtype)


def _attn_kernel(qT_ref, k_ref, vT_ref, lq1_ref, lk1_ref, lq2_ref, lk2_ref, g_ref, o_ref,
                 *scratch, lam_init):
    nq = ATTN_Q_TILES
    n_grp = 2 * nq
    acc_refs = scratch[0:n_grp]
    m_refs = scratch[n_grp:2 * n_grp]
    sa_refs = list(zip(scratch[2 * n_grp:3 * n_grp], scratch[4 * n_grp:5 * n_grp]))
    sb_refs = list(zip(scratch[3 * n_grp:4 * n_grp], scratch[5 * n_grp:6 * n_grp]))
    i = pl.program_id(1)
    t = qT_ref.shape[3]
    row = lax.broadcasted_iota(jnp.int32, (qT_ref.shape[2], t), 0)
    q_grp = []
    for mp in range(2):
        keep = (row < DA_QK_DIM) if mp == 0 else (row >= DA_QK_DIM)
        for u in range(nq):
            q = qT_ref[0, u]
            q_grp.append(jnp.where(keep, q, jnp.zeros_like(q)))

    for gi in range(n_grp):
        m_refs[gi][...] = jnp.full_like(m_refs[gi], -jnp.inf)
        acc_refs[gi][...] = jnp.zeros_like(acc_refs[gi])

    all_groups = tuple(range(n_grp))

    def scores(j, bufs, diag, groups=all_groups):
        kt = k_ref[pl.ds(pl.multiple_of(j * t, t), t), :]
        for gi in groups:
            if diag is None or gi % nq >= diag:
                s = _dot(kt, q_grp[gi])
                s_ref, smax_ref = bufs[gi]
                s_ref[...] = s
                smax_ref[...] = jnp.max(s, axis=0, keepdims=True)

    def update(bufs, j, diag, groups=all_groups):
        vt = vT_ref[0, j]
        for gi in groups:
            u = gi % nq
            if diag is not None and u < diag:
                continue
            s_ref, smax_ref = bufs[gi]
            s = s_ref[...]
            if diag is not None and u == diag:
                kc = lax.broadcasted_iota(jnp.int32, s.shape, 0) // CHUNK
                qc = lax.broadcasted_iota(jnp.int32, s.shape, 1) // CHUNK
                s = jnp.where(kc <= qc, s, -jnp.inf)
                s_max = jnp.max(s, axis=0, keepdims=True)
            else:
                s_max = smax_ref[...]
            m_prev = m_refs[gi][...]
            m_new = jnp.maximum(m_prev, s_max)
            alpha = jnp.exp2(m_prev - m_new)
            p = jnp.exp2(s - m_new).astype(BF16)
            acc_refs[gi][...] = alpha * acc_refs[gi][...] + _dot(vt, p)
            m_refs[gi][...] = m_new

    scores(0, sa_refs, None)

    def pair(j):
        for gi in all_groups:
            scores(j + 1, sb_refs, None, (gi,))
            update(sa_refs, j, None, (gi,))
        for gi in all_groups:
            scores(j + 2, sa_refs, None, (gi,))
            update(sb_refs, j + 1, None, (gi,))

    assert nq == 4

    def double_pair(j):
        pair(j)
        pair(j + 2)

    def trip(g, carry):
        double_pair(8 * g)
        double_pair(8 * g + 4)
        return carry

    lax.fori_loop(0, i // 2, trip, 0)

    @pl.when(i % 2 == 1)
    def _():
        double_pair(4 * (i - 1))

    first_diag = nq * i
    bufs = (sa_refs, sb_refs)
    for e in range(nq):
        for gi in all_groups:
            if e + 1 < nq:
                scores(first_diag + e + 1, bufs[(e + 1) % 2], e + 1, (gi,))
            update(bufs[e % 2], first_diag + e, e, (gi,))

    lam = (jnp.exp(jnp.sum(lq1_ref[...] * lk1_ref[...], axis=-1, keepdims=True))
           - jnp.exp(jnp.sum(lq2_ref[...] * lk2_ref[...], axis=-1, keepdims=True))
           + lam_init)
    for u in range(nq):
        a1 = acc_refs[u][...]
        a2 = acc_refs[nq + u][...]
        ao = (a1[:DA_V_DIM, :] / a1[DA_V_DIM:DA_V_DIM + 1, :]
              - lam * (a2[:DA_V_DIM, :] / a2[DA_V_DIM:DA_V_DIM + 1, :]))
        y = ao * lax.rsqrt(jnp.mean(ao * ao, axis=0, keepdims=True) + EPS)
        o_ref[u * t:(u + 1) * t, :] = (y.T * g_ref[...] * (1.0 - lam_init)).astype(o_ref.dtype)


def _split3(x):
    hi = x.astype(BF16).astype(F32)
    r = x - hi
    mid = r.astype(BF16).astype(F32)
    lo = (r - mid).astype(BF16).astype(F32)
    return hi, mid, lo


def _mlstm_block(qT, k, vT, om, gates, g_row, state_ref, m_ref):
    L = k.shape[0]
    s_idx = lax.broadcasted_iota(jnp.int32, (L, L), 0)
    t_idx = lax.broadcasted_iota(jnp.int32, (L, L), 1)
    causal = s_idx <= t_idx
    tri = jnp.where(t_idx <= s_idx, 1.0, 0.0).astype(BF16)
    parts = _dot(tri, jnp.concatenate(_split3(gates), axis=1).astype(BF16))
    cum = parts[:, :LANES] + parts[:, LANES:2 * LANES] + parts[:, 2 * LANES:]
    gatesT = gates.T
    cumT = cum.T

    heads = range(ML_HEADS)
    sl = [slice(h * ML_DIM, (h + 1) * ML_DIM) for h in heads]
    state = [state_ref[h] for h in heads]
    b_row = [cumT[ML_HEADS + h:ML_HEADS + h + 1, :] for h in heads]
    i_row = [gatesT[h:h + 1, :] for h in heads]
    m_prev = [m_ref[h, 0:1, 0:1] for h in heads]

    s_raw = [_dot(k[:, sl[h]], qT[h]) for h in heads]
    inter_nd = [_dot(state[h].astype(BF16), qT[h]) for h in heads]

    m_t, w_inter, sT = [], [], []
    for h in heads:
        c_col = gates[:, h:h + 1] - cum[:, ML_HEADS + h:ML_HEADS + h + 1]
        d = jnp.where(causal, c_col + b_row[h], -jnp.inf)
        inter = b_row[h] + m_prev[h]
        m_t.append(jnp.maximum(inter, jnp.max(d, axis=0, keepdims=True)))
        w_inter.append(jnp.exp(inter - m_t[h]))
        sT.append((s_raw[h] * jnp.exp(d - m_t[h])).astype(BF16))

    outs = []
    for h in heads:
        nd = w_inter[h] * inter_nd[h] + _dot(vT[h], sT[h])
        num = nd[:ML_DIM, :]
        den = nd[ML_DIM:ML_DIM + 1, :]
        hh = num / jnp.maximum(jnp.abs(den), jnp.exp(-m_t[h]))
        mu = jnp.mean(hh, axis=0, keepdims=True)
        xc = hh - mu
        var = jnp.mean(xc * xc, axis=0, keepdims=True)
        y = (xc * lax.rsqrt(var + EPS)).T * g_row
        outs.append(y * _sigmoid(om[:, sl[h]]))

    for h in heads:
        b_last = b_row[h][:, L - 1:L]
        g_row = b_last - b_row[h] + i_row[h]
        m_new = jnp.maximum(b_last + m_prev[h], jnp.max(g_row, axis=1, keepdims=True))
        wk = jnp.exp(g_row - m_new)
        decay = jnp.exp(b_last + m_prev[h] - m_new)
        vTw = (vT[h].astype(F32) * wk).astype(BF16)
        state_ref[h] = decay * state[h] + _dot(vTw, k[:, sl[h]])
        m_ref[h] = jnp.broadcast_to(m_new, m_ref.shape[1:])
    return outs


def _gelu_tanh(x):
    return 0.5 * x * (1.0 + jnp.tanh(math.sqrt(2.0 / math.pi) * (x + 0.044715 * (x * x * x))))


def _post_kernel(x_ref, ao_ref, hm_ref, ga_ref, gb_ref, p_ref, wya_ref, wyb_ref, wo_ref, n2g_ref,
                 wup_ref, convw_ref, convb_ref, wdown_ref, wple_ref, wpg_ref, fg_ref,
                 o_ref, carry_ref, abuf_ref, u_ref):
    i = pl.program_id(0)
    tm = x_ref.shape[0]
    fc = abuf_ref.shape[1]
    d_ff = wdown_ref.shape[0]
    n_chunks = d_ff // fc

    @pl.when(i == 0)
    def _():
        carry_ref[...] = jnp.zeros_like(carry_ref)

    ya = _dot(ao_ref[...], wya_ref[...])
    yb = _dot(hm_ref[...], wyb_ref[...])
    merged = _sigmoid(ga_ref[...].astype(F32)) * ya + _sigmoid(gb_ref[...].astype(F32)) * yb
    x1 = x_ref[...] + _dot(merged.astype(BF16), wo_ref[...])

    h2 = _rmsnorm(x1, n2g_ref[...]).astype(BF16)
    def up(c):
        return (_dot(h2, wup_ref[:, c * fc:(c + 1) * fc]),
                _dot(h2, wup_ref[:, d_ff + c * fc:d_ff + (c + 1) * fc]))

    nxt = up(0)
    for c in range(n_chunks):
        cs = slice(c * fc, (c + 1) * fc)
        a, g = nxt
        if c + 1 < n_chunks:
            nxt = up(c + 1)
        abuf_ref[0:SUBLANES, :] = carry_ref[c]
        abuf_ref[SUBLANES:, :] = a
        carry_ref[c] = a[tm - SUBLANES:, :]
        cw = convw_ref[:, cs]
        ac = (cw[2:3, :] * a
              + cw[1:2, :] * abuf_ref[pl.ds(SUBLANES - 1, tm), :]
              + cw[0:1, :] * abuf_ref[pl.ds(SUBLANES - 2, tm), :]
              + convb_ref[:, cs])
        u_ref[:, cs] = (_gelu_tanh(ac) * g).astype(BF16)
    x2 = x1 + _dot(u_ref[...], wdown_ref[...])

    pe = _dot(p_ref[...].astype(BF16), wple_ref[...])
    pg = _dot(x2.astype(BF16), wpg_ref[...])
    x3 = x2 + pe * _sigmoid(pg)
    o_ref[...] = _rmsnorm(x3, fg_ref[...])


def _compiler_params(n_axes):
    return pltpu.CompilerParams(dimension_semantics=("arbitrary",) * n_axes,
                                vmem_limit_bytes=VMEM_LIMIT_BYTES)


def _rope_tables(S, tm):
    half = DA_QK_DIM // 2
    reps = LANES // half
    inv_freq = ROPE_THETA ** (-jnp.arange(0, DA_QK_DIM, 2, dtype=F32) / DA_QK_DIM)
    freq = jnp.tile(inv_freq, reps)[None, :]
    sign = jnp.tile(jnp.concatenate([-jnp.ones(half, F32), jnp.ones(half, F32)]), reps // 2)[None, :]
    in_tile = jnp.arange(tm, dtype=F32)[:, None] * freq
    base = (jnp.arange(S // tm, dtype=F32) * tm)[:, None] * freq
    cb, sb = jnp.cos(base), jnp.sin(base)
    zeros = jnp.zeros_like(cb)
    base_rows = jnp.stack([cb, sb, sign * cb, sign * sb] + [zeros] * (SUBLANES - 4), axis=1)
    return base_rows, jnp.cos(in_tile), jnp.sin(in_tile)


def kernel(x, p, norm1_g, w_in, b_if, conv_m_w, lam_q1, lam_k1, lam_q2, lam_k2, da_norm_g,
           ml_norm_g, w_ya, w_yb, w_o, norm2_g, w_up, conv_f_w, conv_f_b, w_down, w_ple, w_pg,
           final_g):
    B, S, D = x.shape
    depth = w_in.shape[0]
    assert B == 1 and depth == 1
    d_ff = w_down.shape[1]
    qk_w = DA_HEADS * 2 * DA_QK_DIM
    v_w = DA_HEADS * DA_V_DIM
    ml_w = ML_HEADS * ML_DIM
    lam_init = 0.8 - 0.6 * math.exp(-0.3 * 0)

    tm1 = min(PROJ_ROWS, S)
    tile = min(ATTN_TILE, S)
    ml_blk = min(ML_BLOCK, S)
    tm4 = min(POST_ROWS, S)
    assert S % tm1 == 0 and S % tile == 0 and S % tm4 == 0 and tm1 % tile == 0
    assert ml_blk == tile and DA_V_DIM == ML_DIM
    assert d_ff % FF_CHUNK == 0
    n_tiles = S // tile
    tq = ATTN_Q_TILES * tile
    n_grp = 2 * ATTN_Q_TILES
    assert S % tq == 0
    v_rows = DA_V_DIM + V_PAD_ROWS

    x2d = x[0]
    wi = w_in[0]
    if_lo = 2 * qk_w + v_w + 4 * ml_w
    if_hi = if_lo + 2 * ML_HEADS
    w_main = wi[:, :if_lo].astype(BF16)
    w_if = jnp.pad(wi[:, if_lo:if_hi], ((0, 0), (0, LANES - 2 * ML_HEADS))).astype(BF16)
    w_g = wi[:, if_hi:].astype(BF16)
    bif = jnp.pad(b_if[0], (0, LANES - 2 * ML_HEADS))[None, :]
    rope_base, rope_cos, rope_sin = _rope_tables(S, tm1)

    row1 = lambda a: a.reshape(1, -1)

    proj_out_shapes = (
        jax.ShapeDtypeStruct((DA_HEADS, n_tiles, LANES, tile), BF16),
        jax.ShapeDtypeStruct((S, qk_w), BF16),
        jax.ShapeDtypeStruct((DA_HEADS, n_tiles, v_rows, tile), BF16),
        jax.ShapeDtypeStruct((S, ml_w), BF16),
        jax.ShapeDtypeStruct((S, D), BF16),
        jax.ShapeDtypeStruct((S, D), BF16),
    )
    rows = lambda w: pl.BlockSpec((tm1, w), lambda i: (i, 0))
    tspec = pl.BlockSpec((DA_HEADS, tm1 // tile, LANES, tile), lambda i: (0, i, 0, 0))
    vspec = pl.BlockSpec((DA_HEADS, tm1 // tile, v_rows, tile), lambda i: (0, i, 0, 0))
    qT, k_a, vT, hm, ga, gb = pl.pallas_call(
        functools.partial(_proj_kernel, tile=tile),
        grid=(S // tm1,),
        in_specs=[rows(D), _const_spec((1, D)),
                  pl.BlockSpec((1, SUBLANES, LANES), lambda i: (i, 0, 0)),
                  _const_spec((tm1, LANES)), _const_spec((tm1, LANES)),
                  _const_spec(w_main.shape), _const_spec(w_if.shape), _const_spec(w_g.shape),
                  _const_spec((1, LANES)), _const_spec(conv_m_w.shape[1:]),
                  _const_spec((1, ML_DIM))],
        out_specs=(tspec, rows(qk_w), vspec, rows(ml_w), rows(D), rows(D)),
        out_shape=proj_out_shapes,
        scratch_shapes=[pltpu.VMEM((SUBLANES, 2 * ml_w), F32),
                        pltpu.VMEM((tm1 + SUBLANES, 2 * ml_w), F32),
                        pltpu.VMEM((ML_HEADS, v_rows, ML_DIM), F32),
                        pltpu.VMEM((ML_HEADS, SUBLANES, LANES), F32)],
        compiler_params=_compiler_params(1),
        name="proj",
    )(x2d, row1(norm1_g[0]), rope_base, rope_cos, rope_sin, w_main, w_if, w_g, bif, conv_m_w[0],
      row1(ml_norm_g[0]))

    ao = pl.pallas_call(
        functools.partial(_attn_kernel, lam_init=lam_init),
        grid=(DA_HEADS, S // tq),
        in_specs=[pl.BlockSpec((1, ATTN_Q_TILES, LANES, tile), lambda h, i: (h, i, 0, 0)),
                  pl.BlockSpec((S, LANES), lambda h, i: (0, h)),
                  pl.BlockSpec((1, n_tiles, v_rows, tile), lambda h, i: (h, 0, 0, 0)),
                  _const_spec((1, DA_QK_DIM)), _const_spec((1, DA_QK_DIM)),
                  _const_spec((1, DA_QK_DIM)), _const_spec((1, DA_QK_DIM)),
                  _const_spec((1, DA_V_DIM))],
        out_specs=pl.BlockSpec((tq, LANES), lambda h, i: (i, h)),
        out_shape=jax.ShapeDtypeStruct((S, v_w), BF16),
        scratch_shapes=([pltpu.VMEM((v_rows, tile), F32)] * n_grp
                        + [pltpu.VMEM((1, tile), F32)] * n_grp
                        + [pltpu.VMEM((tile, tile), F32)] * (2 * n_grp)
                        + [pltpu.VMEM((1, tile), F32)] * (2 * n_grp)),
        compiler_params=_compiler_params(2),
        name="diff_attn",
    )(qT, k_a, vT, row1(lam_q1[0]), row1(lam_k1[0]), row1(lam_q2[0]), row1(lam_k2[0]),
      row1(da_norm_g[0]))

    n_chunks = d_ff // FF_CHUNK
    conv_w = conv_f_w[0]
    conv_b = row1(conv_f_b[0])
    w_dn = w_down[0].astype(BF16)
    prow = lambda w: pl.BlockSpec((tm4, w), lambda i: (i, 0))
    consts = (w_ya[0].astype(BF16), w_yb[0].astype(BF16), w_o[0].astype(BF16), row1(norm2_g[0]),
              w_up[0].astype(BF16), conv_w, conv_b, w_dn, w_ple[0].astype(BF16), w_pg[0].astype(BF16),
              row1(final_g))
    out = pl.pallas_call(
        _post_kernel,
        grid=(S // tm4,),
        in_specs=[prow(D), prow(v_w), prow(ml_w), prow(D), prow(D), prow(p.shape[-1])]
                 + [_const_spec(a.shape) for a in consts],
        out_specs=prow(D),
        out_shape=jax.ShapeDtypeStruct((S, D), x.dtype),
        scratch_shapes=[pltpu.VMEM((n_chunks, SUBLANES, FF_CHUNK), F32),
                        pltpu.VMEM((tm4 + SUBLANES, FF_CHUNK), F32),
                        pltpu.VMEM((tm4, d_ff), BF16)],
        compiler_params=_compiler_params(1),
        name="post",
    )(x2d, ao, hm, ga, gb, p[0, 0], *consts)
    return out[None]
```

```python
import functools
import math

import jax
import jax.numpy as jnp
from jax import lax
from jax.experimental import pallas as pl
from jax.experimental.pallas import tpu as pltpu

F32 = jnp.float32
BF16 = jnp.bfloat16

CHUNK = 64
ROPE_THETA = 10000.0
DA_HEADS = 4
DA_QK_DIM = 64
DA_V_DIM = 128
ML_HEADS = 4
ML_DIM = 128
EPS = 1e-6
LOG2E = math.log2(math.e)
V_PAD_ROWS = 16
LANES = 128
SUBLANES = 8
VMEM_LIMIT_BYTES = 56 * 1024 * 1024

PROJ_ROWS = 512
ATTN_TILE = 256
ATTN_Q_TILES = 4
ML_BLOCK = 256
POST_ROWS = 512
FF_CHUNK = 256


def _dot(a, b):
    return jnp.dot(a, b, preferred_element_type=F32)


def _sigmoid(x):
    return 1.0 / (1.0 + jnp.exp(-x))


def _rmsnorm(x, g):
    return x * lax.rsqrt(jnp.mean(x * x, axis=-1, keepdims=True) + EPS) * g


def _const_spec(shape):
    nd = len(shape)
    return pl.BlockSpec(shape, lambda *_: (0,) * nd)


def _proj_kernel(x_ref, g1_ref, rope_base_ref, rope_cos_ref, rope_sin_ref, w_ref,
                 bif_ref, convw_ref, mlg_ref,
                 qT_ref, k_ref, vT_ref, hm_ref, ga_ref, gb_ref,
                 carry_ref, cbuf_ref, state_ref, m_ref, wif_ref, wg_ref, *, tile):
    i = pl.program_id(0)

    @pl.when(i == 0)
    def _():
        n_gate = 2 * ML_HEADS
        tail = w_ref.shape[1] - wg_ref.shape[1] - n_gate
        gate_cols = w_ref[:, tail:tail + LANES]
        gate_lane = lax.broadcasted_iota(jnp.int32, gate_cols.shape, 1)
        wif_ref[...] = jnp.where(gate_lane < n_gate, gate_cols, jnp.zeros_like(gate_cols))
        wg_ref[...] = w_ref[:, tail + n_gate:]
    tm = x_ref.shape[0]
    n_sub = tm // tile
    hb = _rmsnorm(x_ref[...], g1_ref[...]).astype(BF16)

    qk_w = 2 * DA_HEADS * 2 * DA_QK_DIM
    v_w = DA_HEADS * DA_V_DIM
    ml_w = ML_HEADS * ML_DIM
    d_model = x_ref.shape[1]
    bounds = [0]
    for width in (qk_w, v_w, 2 * ml_w, ml_w, ml_w):
        bounds.append(bounds[-1] + width)

    def proj(n):
        return _dot(hb, w_ref[:, bounds[n]:bounds[n + 1]])

    pad_row = lax.broadcasted_iota(jnp.int32, (V_PAD_ROWS, tm), 0)
    ones_rows = jnp.where(pad_row == 0, 1.0, 0.0).astype(BF16)

    def transposed(cols, extra_rows=None):
        cT = cols.astype(BF16).T
        return cT if extra_rows is None else jnp.concatenate([cT, extra_rows], axis=0)

    def put_transposed(dst_ref, h, cols, extra_rows=None):
        cT = transposed(cols, extra_rows)
        for s in range(n_sub):
            dst_ref[h, s] = cT[:, s * tile:(s + 1) * tile]

    lane = lax.broadcasted_iota(jnp.int32, (tm, LANES), 1)
    gp = _dot(hb, wif_ref[...]) + bif_ref[...]
    logsig = jnp.minimum(gp, 0.0) - jnp.log1p(jnp.exp(-jnp.abs(gp)))
    gates = jnp.where(lane < ML_HEADS, gp, logsig)

    qk = proj(0)
    base = rope_base_ref[0]
    cos_r = rope_cos_ref[...]
    sin_r = rope_sin_ref[...]
    cos = base[0:1, :] * cos_r - base[1:2, :] * sin_r
    sin = base[3:4, :] * cos_r + base[2:3, :] * sin_r
    first_half = (lane % DA_QK_DIM) < (DA_QK_DIM // 2)
    n_grp = DA_HEADS
    for c in range(2 * n_grp):
        t = qk[:, c * LANES:(c + 1) * LANES]
        partner = jnp.where(first_half, pltpu.roll(t, LANES - DA_QK_DIM // 2, 1),
                            pltpu.roll(t, DA_QK_DIM // 2, 1))
        r = t * cos + partner * sin
        if c < n_grp:
            put_transposed(qT_ref, c, r * (DA_QK_DIM ** -0.5 * LOG2E))
        else:
            k_ref[:, (c - n_grp) * LANES:(c - n_grp + 1) * LANES] = r.astype(BF16)

    v = proj(1)
    for h in range(DA_HEADS):
        put_transposed(vT_ref, h, v[:, h * LANES:(h + 1) * LANES], ones_rows)

    @pl.when(i == 0)
    def _():
        carry_ref[...] = jnp.zeros_like(carry_ref)
        state_ref[...] = jnp.zeros_like(state_ref)
        m_ref[...] = jnp.zeros_like(m_ref)

    raw = proj(2)
    vm = proj(3)
    cbuf_ref[0:SUBLANES, :] = carry_ref[...]
    cbuf_ref[SUBLANES:, :] = raw
    carry_ref[...] = raw[tm - SUBLANES:, :]
    cw = convw_ref[...]
    y = (cw[3:4, :] * raw
         + cw[2:3, :] * cbuf_ref[pl.ds(SUBLANES - 1, tm), :]
         + cw[1:2, :] * cbuf_ref[pl.ds(SUBLANES - 2, tm), :]
         + cw[0:1, :] * cbuf_ref[pl.ds(SUBLANES - 3, tm), :])
    y = y * _sigmoid(y)
    om = proj(4)

    km = (y[:, ml_w:] * (ML_DIM ** -0.5)).astype(BF16)
    qmT = [transposed(y[:, h * ML_DIM:(h + 1) * ML_DIM]) for h in range(ML_HEADS)]
    vmT = [transposed(vm[:, h * ML_DIM:(h + 1) * ML_DIM], ones_rows) for h in range(ML_HEADS)]
    for s in range(n_sub):
        if s == 0:
            ga_ref[...] = _dot(hb, wg_ref[:, :d_model]).astype(BF16)
        if s == n_sub - 1:
            gb_ref[...] = _dot(hb, wg_ref[:, d_model:]).astype(BF16)
        rs = slice(s * tile, (s + 1) * tile)
        outs = _mlstm_block([q[:, rs] for q in qmT], km[rs, :], [v[:, rs] for v in vmT],
                            om[rs, :], gates[rs, :], mlg_ref[...], state_ref, m_ref)
        for h in range(ML_HEADS):
            hm_ref[rs, h * ML_DIM:(h + 1) * ML_DIM] = outs[h].astype(hm_ref.dtype)


def _attn_kernel(qT_ref, k_ref, vT_ref, lq1_ref, lk1_ref, lq2_ref, lk2_ref, g_ref, o_ref,
                 *scratch, lam_init):
    nq = ATTN_Q_TILES
    n_grp = 2 * nq
    acc_refs = scratch[0:n_grp]
    m_refs = scratch[n_grp:2 * n_grp]
    sa_refs = list(zip(scratch[2 * n_grp:3 * n_grp], scratch[4 * n_grp:5 * n_grp]))
    sb_refs = list(zip(scratch[3 * n_grp:4 * n_grp], scratch[5 * n_grp:6 * n_grp]))
    i = pl.program_id(1)
    t = qT_ref.shape[3]
    row = lax.broadcasted_iota(jnp.int32, (qT_ref.shape[2], t), 0)
    q_grp = []
    for mp in range(2):
        keep = (row < DA_QK_DIM) if mp == 0 else (row >= DA_QK_DIM)
        for u in range(nq):
            q = qT_ref[0, u]
            q_grp.append(jnp.where(keep, q, jnp.zeros_like(q)))

    for gi in range(n_grp):
        m_refs[gi][...] = jnp.full_like(m_refs[gi], -jnp.inf)
        acc_refs[gi][...] = jnp.zeros_like(acc_refs[gi])

    all_groups = tuple(range(n_grp))

    def scores(j, bufs, diag, groups=all_groups):
        kt = k_ref[pl.ds(pl.multiple_of(j * t, t), t), :]
        for gi in groups:
            if diag is None or gi % nq >= diag:
                s = _dot(kt, q_grp[gi])
                s_ref, smax_ref = bufs[gi]
                s_ref[...] = s
                smax_ref[...] = jnp.max(s, axis=0, keepdims=True)

    def update(bufs, j, diag, groups=all_groups):
        vt = vT_ref[0, j]
        for gi in groups:
            u = gi % nq
            if diag is not None and u < diag:
                continue
            s_ref, smax_ref = bufs[gi]
            s = s_ref[...]
            if diag is not None and u == diag:
                kc = lax.broadcasted_iota(jnp.int32, s.shape, 0) // CHUNK
                qc = lax.broadcasted_iota(jnp.int32, s.shape, 1) // CHUNK
                s = jnp.where(kc <= qc, s, -jnp.inf)
                s_max = jnp.max(s, axis=0, keepdims=True)
            else:
                s_max = smax_ref[...]
            m_prev = m_refs[gi][...]
            m_new = jnp.maximum(m_prev, s_max)
            alpha = jnp.exp2(m_prev - m_new)
            p = jnp.exp2(s - m_new).astype(BF16)
            acc_refs[gi][...] = alpha * acc_refs[gi][...] + _dot(vt, p)
            m_refs[gi][...] = m_new

    scores(0, sa_refs, None)

    def pair(j):
        for gi in all_groups:
            scores(j + 1, sb_refs, None, (gi,))
            update(sa_refs, j, None, (gi,))
        for gi in all_groups:
            scores(j + 2, sa_refs, None, (gi,))
            update(sb_refs, j + 1, None, (gi,))

    assert nq == 4

    def double_pair(j):
        pair(j)
        pair(j + 2)

    def trip(g, carry):
        double_pair(8 * g)
        double_pair(8 * g + 4)
        return carry

    lax.fori_loop(0, i // 2, trip, 0)

    @pl.when(i % 2 == 1)
    def _():
        double_pair(4 * (i - 1))

    first_diag = nq * i
    bufs = (sa_refs, sb_refs)
    for e in range(nq):
        for gi in all_groups:
            if e + 1 < nq:
                scores(first_diag + e + 1, bufs[(e + 1) % 2], e + 1, (gi,))
            update(bufs[e % 2], first_diag + e, e, (gi,))

    lam = (jnp.exp(jnp.sum(lq1_ref[...] * lk1_ref[...], axis=-1, keepdims=True))
           - jnp.exp(jnp.sum(lq2_ref[...] * lk2_ref[...], axis=-1, keepdims=True))
           + lam_init)
    for u in range(nq):
        a1 = acc_refs[u][...]
        a2 = acc_refs[nq + u][...]
        ao = (a1[:DA_V_DIM, :] / a1[DA_V_DIM:DA_V_DIM + 1, :]
              - lam * (a2[:DA_V_DIM, :] / a2[DA_V_DIM:DA_V_DIM + 1, :]))
        y = ao * lax.rsqrt(jnp.mean(ao * ao, axis=0, keepdims=True) + EPS)
        o_ref[u * t:(u + 1) * t, :] = (y.T * g_ref[...] * (1.0 - lam_init)).astype(o_ref.dtype)


def _split3(x):
    hi = x.astype(BF16).astype(F32)
    r = x - hi
    mid = r.astype(BF16).astype(F32)
    lo = (r - mid).astype(BF16).astype(F32)
    return hi, mid, lo


def _mlstm_block(qT, k, vT, om, gates, g_row, state_ref, m_ref):
    L = k.shape[0]
    s_idx = lax.broadcasted_iota(jnp.int32, (L, L), 0)
    t_idx = lax.broadcasted_iota(jnp.int32, (L, L), 1)
    causal = s_idx <= t_idx
    tri = jnp.where(t_idx <= s_idx, 1.0, 0.0).astype(BF16)
    parts = _dot(tri, jnp.concatenate(_split3(gates), axis=1).astype(BF16))
    cum = parts[:, :LANES] + parts[:, LANES:2 * LANES] + parts[:, 2 * LANES:]
    gatesT = gates.T
    cumT = cum.T

    heads = range(ML_HEADS)
    sl = [slice(h * ML_DIM, (h + 1) * ML_DIM) for h in heads]
    state = [state_ref[h] for h in heads]
    b_row = [cumT[ML_HEADS + h:ML_HEADS + h + 1, :] for h in heads]
    i_row = [gatesT[h:h + 1, :] for h in heads]
    m_prev = [m_ref[h, 0:1, 0:1] for h in heads]

    s_raw = [_dot(k[:, sl[h]], qT[h]) for h in heads]
    inter_nd = [_dot(state[h].astype(BF16), qT[h]) for h in heads]

    m_t, w_inter, sT = [], [], []
    for h in heads:
        c_col = gates[:, h:h + 1] - cum[:, ML_HEADS + h:ML_HEADS + h + 1]
        d = jnp.where(causal, c_col + b_row[h], -jnp.inf)
        inter = b_row[h] + m_prev[h]
        m_t.append(jnp.maximum(inter, jnp.max(d, axis=0, keepdims=True)))
        w_inter.append(jnp.exp(inter - m_t[h]))
        sT.append((s_raw[h] * jnp.exp(d - m_t[h])).astype(BF16))

    outs = []
    for h in heads:
        nd = w_inter[h] * inter_nd[h] + _dot(vT[h], sT[h])
        num = nd[:ML_DIM, :]
        den = nd[ML_DIM:ML_DIM + 1, :]
        hh = num / jnp.maximum(jnp.abs(den), jnp.exp(-m_t[h]))
        mu = jnp.mean(hh, axis=0, keepdims=True)
        xc = hh - mu
        var = jnp.mean(xc * xc, axis=0, keepdims=True)
        y = (xc * lax.rsqrt(var + EPS)).T * g_row
        outs.append(y * _sigmoid(om[:, sl[h]]))

    for h in heads:
        b_last = b_row[h][:, L - 1:L]
        g_row = b_last - b_row[h] + i_row[h]
        m_new = jnp.maximum(b_last + m_prev[h], jnp.max(g_row, axis=1, keepdims=True))
        wk = jnp.exp(g_row - m_new)
        decay = jnp.exp(b_last + m_prev[h] - m_new)
        vTw = (vT[h].astype(F32) * wk).astype(BF16)
        state_ref[h] = decay * state[h] + _dot(vTw, k[:, sl[h]])
        m_ref[h] = jnp.broadcast_to(m_new, m_ref.shape[1:])
    return outs


def _gelu_tanh(x):
    return 0.5 * x * (1.0 + jnp.tanh(math.sqrt(2.0 / math.pi) * (x + 0.044715 * (x * x * x))))


def _post_kernel(x_ref, ao_ref, hm_ref, ga_ref, gb_ref, p_ref, wya_ref, wyb_ref, wo_ref, n2g_ref,
                 wup_ref, convw_ref, convb_ref, wdown_ref, wple_ref, wpg_ref, fg_ref,
                 o_ref, carry_ref, abuf_ref, u_ref):
    i = pl.program_id(0)
    tm = x_ref.shape[0]
    fc = abuf_ref.shape[1]
    d_ff = wdown_ref.shape[0]
    n_chunks = d_ff // fc

    @pl.when(i == 0)
    def _():
        carry_ref[...] = jnp.zeros_like(carry_ref)

    ya = _dot(ao_ref[...], wya_ref[...])
    yb = _dot(hm_ref[...], wyb_ref[...])
    merged = _sigmoid(ga_ref[...].astype(F32)) * ya + _sigmoid(gb_ref[...].astype(F32)) * yb
    x1 = x_ref[...] + _dot(merged.astype(BF16), wo_ref[...])

    h2 = _rmsnorm(x1, n2g_ref[...]).astype(BF16)
    def up(c):
        return (_dot(h2, wup_ref[:, c * fc:(c + 1) * fc]),
                _dot(h2, wup_ref[:, d_ff + c * fc:d_ff + (c + 1) * fc]))

    nxt = up(0)
    for c in range(n_chunks):
        cs = slice(c * fc, (c + 1) * fc)
        a, g = nxt
        if c + 1 < n_chunks:
            nxt = up(c + 1)
        abuf_ref[0:SUBLANES, :] = carry_ref[c]
        abuf_ref[SUBLANES:, :] = a
        carry_ref[c] = a[tm - SUBLANES:, :]
        cw = convw_ref[:, cs]
        ac = (cw[2:3, :] * a
              + cw[1:2, :] * abuf_ref[pl.ds(SUBLANES - 1, tm), :]
              + cw[0:1, :] * abuf_ref[pl.ds(SUBLANES - 2, tm), :]
              + convb_ref[:, cs])
        u_ref[:, cs] = (_gelu_tanh(ac) * g).astype(BF16)
    x2 = x1 + _dot(u_ref[...], wdown_ref[...])

    pe = _dot(p_ref[...].astype(BF16), wple_ref[...])
    pg = _dot(x2.astype(BF16), wpg_ref[...])
    x3 = x2 + pe * _sigmoid(pg)
    o_ref[...] = _rmsnorm(x3, fg_ref[...])


def _compiler_params(n_axes):
    return pltpu.CompilerParams(dimension_semantics=("arbitrary",) * n_axes,
                                vmem_limit_bytes=VMEM_LIMIT_BYTES)


def _rope_tables(S, tm):
    half = DA_QK_DIM // 2
    reps = LANES // half
    inv_freq = ROPE_THETA ** (-jnp.arange(0, DA_QK_DIM, 2, dtype=F32) / DA_QK_DIM)
    freq = jnp.tile(inv_freq, reps)[None, :]
    sign = jnp.tile(jnp.concatenate([-jnp.ones(half, F32), jnp.ones(half, F32)]), reps // 2)[None, :]
    in_tile = jnp.arange(tm, dtype=F32)[:, None] * freq
    base = (jnp.arange(S // tm, dtype=F32) * tm)[:, None] * freq
    cb, sb = jnp.cos(base), jnp.sin(base)
    zeros = jnp.zeros_like(cb)
    base_rows = jnp.stack([cb, sb, sign * cb, sign * sb] + [zeros] * (SUBLANES - 4), axis=1)
    return base_rows, jnp.cos(in_tile), jnp.sin(in_tile)


def kernel(x, p, norm1_g, w_in, b_if, conv_m_w, lam_q1, lam_k1, lam_q2, lam_k2, da_norm_g,
           ml_norm_g, w_ya, w_yb, w_o, norm2_g, w_up, conv_f_w, conv_f_b, w_down, w_ple, w_pg,
           final_g):
    B, S, D = x.shape
    depth = w_in.shape[0]
    assert B == 1 and depth == 1
    d_ff = w_down.shape[1]
    qk_w = DA_HEADS * 2 * DA_QK_DIM
    v_w = DA_HEADS * DA_V_DIM
    ml_w = ML_HEADS * ML_DIM
    lam_init = 0.8 - 0.6 * math.exp(-0.3 * 0)

    tm1 = min(PROJ_ROWS, S)
    tile = min(ATTN_TILE, S)
    ml_blk = min(ML_BLOCK, S)
    tm4 = min(POST_ROWS, S)
    assert S % tm1 == 0 and S % tile == 0 and S % tm4 == 0 and tm1 % tile == 0
    assert ml_blk == tile and DA_V_DIM == ML_DIM
    assert d_ff % FF_CHUNK == 0
    n_tiles = S // tile
    tq = ATTN_Q_TILES * tile
    n_grp = 2 * ATTN_Q_TILES
    assert S % tq == 0
    v_rows = DA_V_DIM + V_PAD_ROWS

    x2d = x[0]
    wi = w_in[0]
    w16 = wi.astype(BF16)
    bif = jnp.pad(b_if[0], (0, LANES - 2 * ML_HEADS))[None, :]
    rope_base, rope_cos, rope_sin = _rope_tables(S, tm1)

    row1 = lambda a: a.reshape(1, -1)

    proj_out_shapes = (
        jax.ShapeDtypeStruct((DA_HEADS, n_tiles, LANES, tile), BF16),
        jax.ShapeDtypeStruct((S, qk_w), BF16),
        jax.ShapeDtypeStruct((DA_HEADS, n_tiles, v_rows, tile), BF16),
        jax.ShapeDtypeStruct((S, ml_w), BF16),
        jax.ShapeDtypeStruct((S, D), BF16),
        jax.ShapeDtypeStruct((S, D), BF16),
    )
    rows = lambda w: pl.BlockSpec((tm1, w), lambda i: (i, 0))
    tspec = pl.BlockSpec((DA_HEADS, tm1 // tile, LANES, tile), lambda i: (0, i, 0, 0))
    vspec = pl.BlockSpec((DA_HEADS, tm1 // tile, v_rows, tile), lambda i: (0, i, 0, 0))
    qT, k_a, vT, hm, ga, gb = pl.pallas_call(
        functools.partial(_proj_kernel, tile=tile),
        grid=(S // tm1,),
        in_specs=[rows(D), _const_spec((1, D)),
                  pl.BlockSpec((1, SUBLANES, LANES), lambda i: (i, 0, 0)),
                  _const_spec((tm1, LANES)), _const_spec((tm1, LANES)),
                  _const_spec(w16.shape),
                  _const_spec((1, LANES)), _const_spec(conv_m_w.shape[1:]),
                  _const_spec((1, ML_DIM))],
        out_specs=(tspec, rows(qk_w), vspec, rows(ml_w), rows(D), rows(D)),
        out_shape=proj_out_shapes,
        scratch_shapes=[pltpu.VMEM((SUBLANES, 2 * ml_w), F32),
                        pltpu.VMEM((tm1 + SUBLANES, 2 * ml_w), F32),
                        pltpu.VMEM((ML_HEADS, v_rows, ML_DIM), F32),
                        pltpu.VMEM((ML_HEADS, SUBLANES, LANES), F32),
                        pltpu.VMEM((D, LANES), BF16),
                        pltpu.VMEM((D, 2 * D), BF16)],
        compiler_params=_compiler_params(1),
        name="proj",
    )(x2d, row1(norm1_g[0]), rope_base, rope_cos, rope_sin, w16, bif, conv_m_w[0],
      row1(ml_norm_g[0]))

    ao = pl.pallas_call(
        functools.partial(_attn_kernel, lam_init=lam_init),
        grid=(DA_HEADS, S // tq),
        in_specs=[pl.BlockSpec((1, ATTN_Q_TILES, LANES, tile), lambda h, i: (h, i, 0, 0)),
                  pl.BlockSpec((S, LANES), lambda h, i: (0, h)),
                  pl.BlockSpec((1, n_tiles, v_rows, tile), lambda h, i: (h, 0, 0, 0)),
                  _const_spec((1, DA_QK_DIM)), _const_spec((1, DA_QK_DIM)),
                  _const_spec((1, DA_QK_DIM)), _const_spec((1, DA_QK_DIM)),
                  _const_spec((1, DA_V_DIM))],
        out_specs=pl.BlockSpec((tq, LANES), lambda h, i: (i, h)),
        out_shape=jax.ShapeDtypeStruct((S, v_w), BF16),
        scratch_shapes=([pltpu.VMEM((v_rows, tile), F32)] * n_grp
                        + [pltpu.VMEM((1, tile), F32)] * n_grp
                        + [pltpu.VMEM((tile, tile), F32)] * (2 * n_grp)
                        + [pltpu.VMEM((1, tile), F32)] * (2 * n_grp)),
        compiler_params=_compiler_params(2),
        name="diff_attn",
    )(qT, k_a, vT, row1(lam_q1[0]), row1(lam_k1[0]), row1(lam_q2[0]), row1(lam_k2[0]),
      row1(da_norm_g[0]))

    n_chunks = d_ff // FF_CHUNK
    conv_w = conv_f_w[0]
    conv_b = row1(conv_f_b[0])
    w_dn = w_down[0].astype(BF16)
    prow = lambda w: pl.BlockSpec((tm4, w), lambda i: (i, 0))
    consts = (w_ya[0].astype(BF16), w_yb[0].astype(BF16), w_o[0].astype(BF16), row1(norm2_g[0]),
              w_up[0].astype(BF16), conv_w, conv_b, w_dn, w_ple[0].astype(BF16), w_pg[0].astype(BF16),
              row1(final_g))
    out = pl.pallas_call(
        _post_kernel,
        grid=(S // tm4,),
        in_specs=[prow(D), prow(v_w), prow(ml_w), prow(D), prow(D), prow(p.shape[-1])]
                 + [_const_spec(a.shape) for a in consts],
        out_specs=prow(D),
        out_shape=jax.ShapeDtypeStruct((S, D), x.dtype),
        scratch_shapes=[pltpu.VMEM((n_chunks, SUBLANES, FF_CHUNK), F32),
                        pltpu.VMEM((tm4 + SUBLANES, FF_CHUNK), F32),
                        pltpu.VMEM((tm4, d_ff), BF16)],
        compiler_params=_compiler_params(1),
        name="post",
    )(x2d, ao, hm, ga, gb, p[0, 0], *consts)
    return out[None]
```
